```python
import math
import jax, jax.numpy as jnp
from jax import lax
import numpy as np

D_MODEL = 1024
BATCH = 4
SEQ = 4096
DEPTH = 2

N_MIXERS = 2
N_CONV_LAYERS = (DEPTH + 1) // 2
N_MLA_LAYERS = DEPTH // 2
NORM_EPS = 1e-6
CONV_WIDTH = 31
MLA_HEADS = 16
QK_NOPE_DIM = 64
QK_ROPE_DIM = 32
V_HEAD_DIM = 64
Q_LORA_RANK = 384
KV_LORA_RANK = 256
ROPE_THETA = 10000.0
Q_BLOCK = 128
N_MEM = 256
XA_HEADS = 4
XA_HEAD_DIM = D_MODEL // XA_HEADS
N_GROUPS = 4
EXPERTS_PER_GROUP = 8
N_EXPERTS = N_GROUPS * EXPERTS_PER_GROUP
TOP_K_IN_GROUP = 2
EXPERT_FF = 256

kernel_name = "hybrid_conformer_mla_memxattn_hmoe"


def rms_norm(x, g):
    xf = x.astype(jnp.float32)
    y = xf * lax.rsqrt(jnp.mean(xf * xf, axis=-1, keepdims=True) + NORM_EPS)
    return (y * g.astype(jnp.float32)).astype(x.dtype)


def layer_norm(x, g, b):
    xf = x.astype(jnp.float32)
    mu = jnp.mean(xf, axis=-1, keepdims=True)
    var = jnp.mean(jnp.square(xf - mu), axis=-1, keepdims=True)
    y = (xf - mu) * lax.rsqrt(var + NORM_EPS)
    return (y * g.astype(jnp.float32) + b.astype(jnp.float32)).astype(x.dtype)


def conformer_conv(h, w_in, b_in, w_dw, b_dw, ln_g, ln_b, w_out, b_out):
    d = h.shape[-1]
    u = h @ w_in + b_in
    a, gate = jnp.split(u, 2, axis=-1)
    u = a * jax.nn.sigmoid(gate)
    u = lax.conv_general_dilated(
        u, w_dw.reshape(CONV_WIDTH, 1, d).astype(u.dtype),
        window_strides=(1,), padding=[(CONV_WIDTH - 1, 0)],
        dimension_numbers=("NWC", "WIO", "NWC"), feature_group_count=d) + b_dw
    u = jax.nn.silu(layer_norm(u, ln_g, ln_b))
    return u @ w_out + b_out


def rope_tables(positions, dim):
    inv_freq = 1.0 / (ROPE_THETA ** (jnp.arange(0, dim, 2, dtype=jnp.float32) / dim))
    ang = positions.astype(jnp.float32)[..., None] * inv_freq
    return jnp.cos(ang), jnp.sin(ang)


def apply_rope(x, cos, sin):
    x1, x2 = jnp.split(x.astype(jnp.float32), 2, axis=-1)
    out = jnp.concatenate([x1 * cos - x2 * sin, x2 * cos + x1 * sin], axis=-1)
    return out.astype(x.dtype)


def causal_block_attention(q, k, v, scale):
    b, s, h, dqk = q.shape
    dv = v.shape[-1]
    nb = s // Q_BLOCK
    qb = q.reshape(b, nb, Q_BLOCK, h, dqk).transpose(1, 0, 2, 3, 4)
    k_idx = jnp.arange(s)

    def one_block(args):
        q_blk, blk = args
        sc = jnp.einsum("bqhd,bkhd->bhqk", q_blk, k).astype(jnp.float32) * scale
        q_idx = blk * Q_BLOCK + jnp.arange(Q_BLOCK)
        mask = q_idx[:, None] >= k_idx[None, :]
        sc = jnp.where(mask[None, None], sc, -jnp.inf)
        p = jax.nn.softmax(sc, axis=-1).astype(v.dtype)
        return jnp.einsum("bhqk,bkhd->bqhd", p, v)

    o = lax.map(one_block, (qb, jnp.arange(nb)))
    return o.transpose(1, 0, 2, 3, 4).reshape(b, s, h * dv)


def mla(h, cos, sin, w_down, q_norm, w_uq, kv_norm, w_ukv, w_o):
    b, s, _ = h.shape
    dn = h @ w_down
    c_q, c_kv, k_r = jnp.split(dn, [Q_LORA_RANK, Q_LORA_RANK + KV_LORA_RANK], axis=-1)
    q = (rms_norm(c_q, q_norm) @ w_uq).reshape(b, s, MLA_HEADS, QK_NOPE_DIM + QK_ROPE_DIM)
    q_nope, q_r = jnp.split(q, [QK_NOPE_DIM], axis=-1)
    q_r = apply_rope(q_r, cos[:, :, None, :], sin[:, :, None, :])
    kv = (rms_norm(c_kv, kv_norm) @ w_ukv).reshape(b, s, MLA_HEADS, QK_NOPE_DIM + V_HEAD_DIM)
    k_nope, v = jnp.split(kv, [QK_NOPE_DIM], axis=-1)
    k_r = apply_rope(k_r, cos, sin)
    k_r = jnp.broadcast_to(k_r[:, :, None, :], (b, s, MLA_HEADS, QK_ROPE_DIM))
    q_full = jnp.concatenate([q_nope, q_r], axis=-1)
    k_full = jnp.concatenate([k_nope, k_r], axis=-1)
    scale = 1.0 / math.sqrt(QK_NOPE_DIM + QK_ROPE_DIM)
    o = causal_block_attention(q_full, k_full, v, scale)
    return o @ w_o


def mem_cross_attention(h, mem_n, wq, wkv, wo):
    b, s, _ = h.shape
    m = mem_n.shape[1]
    q = (h @ wq).reshape(b, s, XA_HEADS, XA_HEAD_DIM)
    k, v = jnp.split(mem_n @ wkv, 2, axis=-1)
    k = k.reshape(b, m, XA_HEADS, XA_HEAD_DIM)
    v = v.reshape(b, m, XA_HEADS, XA_HEAD_DIM)
    sc = jnp.einsum("bshd,bmhd->bhsm", q, k).astype(jnp.float32) / math.sqrt(XA_HEAD_DIM)
    p = jax.nn.softmax(sc, axis=-1).astype(v.dtype)
    o = jnp.einsum("bhsm,bmhd->bshd", p, v).reshape(b, s, D_MODEL)
    return o @ wo


def hierarchical_moe(h, w_grp, b_grp, w_exp, b_exp, w_gate, w_up, w_down):
    shp = h.shape
    xt = h.reshape(-1, shp[-1])
    g_logits = (xt @ w_grp + b_grp).astype(jnp.float32)
    g_probs = jax.nn.softmax(g_logits, axis=-1)
    g_val, g_idx = lax.top_k(g_probs, 1)
    e_logits = (xt @ w_exp + b_exp).astype(jnp.float32).reshape(-1, N_GROUPS, EXPERTS_PER_GROUP)
    e_logits = jnp.take_along_axis(e_logits, g_idx[:, :, None], axis=1)[:, 0]
    e_probs = jax.nn.softmax(e_logits, axis=-1)
    e_val, e_idx = lax.top_k(e_probs, TOP_K_IN_GROUP)
    e_val = e_val / jnp.sum(e_val, axis=-1, keepdims=True)
    within = jnp.sum(jax.nn.one_hot(e_idx, EXPERTS_PER_GROUP, dtype=jnp.float32)
                     * e_val[..., None], axis=1)
    combine = (g_val[:, :, None]
               * jax.nn.one_hot(g_idx[:, 0], N_GROUPS, dtype=jnp.float32)[:, :, None]
               * within[:, None, :]).reshape(-1, N_EXPERTS).astype(h.dtype)
    y = jnp.zeros_like(xt)
    for e in range(N_EXPERTS):
        he = jax.nn.silu(xt @ w_gate[e]) * (xt @ w_up[e])
        y = y + combine[:, e:e + 1] * (he @ w_down[e])
    return y.reshape(shp)


def setup_inputs(seed: int = 0) -> dict:
    key = jax.random.key(seed)
    ks = iter(jax.random.split(key, 64))
    f32 = jnp.float32

    def w(shape, fan_in):
        return jax.random.normal(next(ks), shape, f32) * (fan_in ** -0.5)

    def gain(shape):
        return 1.0 + 0.01 * jax.random.normal(next(ks), shape, f32)

    def bias(shape, scale=0.01):
        return scale * jax.random.normal(next(ks), shape, f32)

    D = D_MODEL
    NC, NM, L = N_CONV_LAYERS, N_MLA_LAYERS, DEPTH
    x = jax.random.normal(next(ks), (BATCH, SEQ, D), f32)
    mem = jax.random.normal(next(ks), (BATCH, N_MEM, D), f32)
    offs = jax.random.randint(next(ks), (BATCH, 1), 0, 1024, dtype=jnp.int32)
    positions = (offs + jnp.arange(SEQ, dtype=jnp.int32)[None, :]).astype(jnp.int32)
    return {
        "x": x, "mem": mem, "positions": positions,
        "norm_mix": gain((L, D)),
        "conv_w_in": w((NC, D, 2 * D), D), "conv_b_in": bias((NC, 2 * D)),
        "conv_w_dw": w((NC, CONV_WIDTH, D), CONV_WIDTH), "conv_b_dw": bias((NC, D)),
        "conv_ln_g": gain((NC, D)), "conv_ln_b": bias((NC, D)),
        "conv_w_out": w((NC, D, D), D), "conv_b_out": bias((NC, D)),
        "mla_w_down": w((NM, D, Q_LORA_RANK + KV_LORA_RANK + QK_ROPE_DIM), D),
        "mla_q_norm": gain((NM, Q_LORA_RANK)),
        "mla_w_uq": w((NM, Q_LORA_RANK, MLA_HEADS * (QK_NOPE_DIM + QK_ROPE_DIM)), Q_LORA_RANK),
        "mla_kv_norm": gain((NM, KV_LORA_RANK)),
        "mla_w_ukv": w((NM, KV_LORA_RANK, MLA_HEADS * (QK_NOPE_DIM + V_HEAD_DIM)), KV_LORA_RANK),
        "mla_w_o": w((NM, MLA_HEADS * V_HEAD_DIM, D), MLA_HEADS * V_HEAD_DIM),
        "norm_xa": gain((L, D)), "norm_mem": gain((L, D)),
        "xa_wq": w((L, D, D), D), "xa_wkv": w((L, D, 2 * D), D), "xa_wo": w((L, D, D), D),
        "norm_ffn": gain((L, D)),
        "moe_w_grp": w((L, D, N_GROUPS), D), "moe_b_grp": bias((L, N_GROUPS)),
        "moe_w_exp": w((L, D, N_EXPERTS), D), "moe_b_exp": bias((L, N_EXPERTS)),
        "moe_w_gate": w((L, N_EXPERTS, D, EXPERT_FF), D),
        "moe_w_up": w((L, N_EXPERTS, D, EXPERT_FF), D),
        "moe_w_down": w((L, N_EXPERTS, EXPERT_FF, D), EXPERT_FF),
        "final_norm": gain((D,)),
    }


def reference(x, mem, positions, norm_mix,
              conv_w_in, conv_b_in, conv_w_dw, conv_b_dw, conv_ln_g, conv_ln_b,
              conv_w_out, conv_b_out,
              mla_w_down, mla_q_norm, mla_w_uq, mla_kv_norm, mla_w_ukv, mla_w_o,
              norm_xa, norm_mem, xa_wq, xa_wkv, xa_wo,
              norm_ffn, moe_w_grp, moe_b_grp, moe_w_exp, moe_b_exp,
              moe_w_gate, moe_w_up, moe_w_down, final_norm):
    cos, sin = rope_tables(positions, QK_ROPE_DIM)
    h = x
    for i in range(DEPTH):
        j = i // N_MIXERS
        hn = rms_norm(h, norm_mix[i])
        if i % N_MIXERS == 0:
            mix = conformer_conv(hn, conv_w_in[j], conv_b_in[j], conv_w_dw[j], conv_b_dw[j],
                                 conv_ln_g[j], conv_ln_b[j], conv_w_out[j], conv_b_out[j])
        else:
            mix = mla(hn, cos, sin, mla_w_down[j], mla_q_norm[j], mla_w_uq[j],
                      mla_kv_norm[j], mla_w_ukv[j], mla_w_o[j])
        h = h + mix
        h = h + mem_cross_attention(rms_norm(h, norm_xa[i]), rms_norm(mem, norm_mem[i]),
                                    xa_wq[i], xa_wkv[i], xa_wo[i])
        h = h + hierarchical_moe(rms_norm(h, norm_ffn[i]), moe_w_grp[i], moe_b_grp[i],
                                 moe_w_exp[i], moe_b_exp[i], moe_w_gate[i], moe_w_up[i],
                                 moe_w_down[i])
    return rms_norm(h, final_norm)
```

```python
import functools
import math

import jax
import jax.numpy as jnp
from jax import lax
from jax.experimental import pallas as pl
from jax.experimental.pallas import tpu as pltpu

F32 = jnp.float32
BF16 = jnp.bfloat16

NORM_EPS = 1e-6
CONV_WIDTH = 31
CONV_HALO = 32
MLA_HEADS = 16
QK_NOPE_DIM = 64
QK_ROPE_DIM = 32
V_HEAD_DIM = 64
Q_LORA_RANK = 384
KV_LORA_RANK = 256
ROPE_THETA = 10000.0
HEAD_PAD = 128
XA_HEADS = 4
N_GROUPS = 4
EXPERTS_PER_GROUP = 8
N_EXPERTS = N_GROUPS * EXPERTS_PER_GROUP
ROUTER_LANES = 128
NEG_BIG = -1e30

VMEM_LIMIT = 56 * 1024 * 1024


def _rms(x, g):
    return x * lax.rsqrt(jnp.mean(x * x, axis=-1, keepdims=True) + NORM_EPS) * g


def _dot(a, b):
    return jnp.dot(a, b, preferred_element_type=F32)


def _dot_nt(a, b):
    return lax.dot_general(a, b, (((1,), (1,)), ((), ())), preferred_element_type=F32)


def _params(*sem):
    return pltpu.CompilerParams(dimension_semantics=sem, vmem_limit_bytes=VMEM_LIMIT)


def _rope_kernel(pos_ref, freq_ref, cos_ref, sin_ref):
    ang = pos_ref[...].astype(F32) * freq_ref[...]
    cos_ref[...] = jnp.cos(ang)
    sin_ref[...] = jnp.sin(ang)


def _rope_tables(positions):
    half = QK_ROPE_DIM // 2
    t = positions.size
    inv_freq = 1.0 / (ROPE_THETA ** (jnp.arange(0, QK_ROPE_DIM, 2, dtype=F32) / QK_ROPE_DIM))
    rows = t * half // 128
    pos_rep = jnp.repeat(positions.reshape(-1), half).reshape(rows, 128)
    freq = jnp.tile(inv_freq, 128 // half).reshape(1, 128)
    cos, sin = pl.pallas_call(
        _rope_kernel,
        out_shape=(jax.ShapeDtypeStruct((rows, 128), F32),) * 2,
        name="rope_tables",
    )(pos_rep, freq)
    cos = cos.reshape(t, half)
    sin = sin.reshape(t, half)
    one = jnp.ones((t, QK_NOPE_DIM), F32)
    z16 = jnp.zeros((t, half), F32)
    z64 = jnp.zeros((t, QK_NOPE_DIM), F32)
    pad1 = jnp.ones((t, HEAD_PAD - QK_NOPE_DIM - QK_ROPE_DIM), F32)
    pad0 = jnp.zeros((t, HEAD_PAD - QK_NOPE_DIM - QK_ROPE_DIM), F32)
    tab_c = jnp.concatenate([one, cos, cos, pad1], axis=1)
    tab_a = jnp.concatenate([z64, sin, z16, pad0], axis=1)
    tab_b = jnp.concatenate([z64, z16, sin, pad0], axis=1)
    return tab_c, tab_a, tab_b


def _rope_apply(x, c, a, b):
    return x * c - pltpu.roll(x, HEAD_PAD - 16, 1) * a + pltpu.roll(x, 16, 1) * b


def _mem_kv_kernel(mem_ref, g_ref, w_ref, o_ref):
    mn = _rms(mem_ref[0], g_ref[0]).astype(BF16)
    o_ref[0, 0] = _dot(mn, w_ref[0]).astype(BF16)


def _mem_kv(mem, norm_mem, wkv):
    b, m, d = mem.shape
    l = norm_mem.shape[0]
    return pl.pallas_call(
        _mem_kv_kernel,
        grid=(l, b),
        in_specs=[
            pl.BlockSpec((1, m, d), lambda i, j: (j, 0, 0)),
            pl.BlockSpec((1, 1, d), lambda i, j: (i, 0, 0)),
            pl.BlockSpec((1, d, 2 * d), lambda i, j: (i, 0, 0)),
        ],
        out_specs=pl.BlockSpec((1, 1, m, 2 * d), lambda i, j: (i, j, 0, 0)),
        out_shape=jax.ShapeDtypeStruct((l, b, m, 2 * d), BF16),
        compiler_params=_params("arbitrary", "arbitrary"),
        name="mem_kv",
    )(mem, norm_mem.reshape(l, 1, d), wkv)


CONV_ROWS = 32


def _conv_mixer_kernel(h_ref, g_ref, win_ref, bin_ref, wdw_ref, bdw_ref, lng_ref, lnb_ref,
                       wout_ref, bout_ref, o_ref, cb_ref, cv_ref):
    ts, d = h_ref.shape[1], h_ref.shape[2]

    @pl.when(pl.program_id(1) == 0)
    def _():
        cb_ref[0:CONV_HALO, :] = jnp.zeros((CONV_HALO, d), F32)

    h = h_ref[0]
    hn = _rms(h, g_ref[...]).astype(BF16)
    u = _dot(hn, win_ref[...]) + bin_ref[...]
    cb_ref[CONV_HALO:CONV_HALO + ts, :] = u[:, :d] * jax.nn.sigmoid(u[:, d:])

    base = CONV_HALO - (CONV_WIDTH - 1)
    for r0 in range(0, ts, CONV_ROWS):
        for c0 in range(0, d, 128):
            acc = jnp.zeros((CONV_ROWS, 128), F32)
            for k in range(CONV_WIDTH):
                acc = acc + cb_ref[r0 + base + k:r0 + base + k + CONV_ROWS, c0:c0 + 128] * wdw_ref[k:k + 1, c0:c0 + 128]
            cv_ref[r0:r0 + CONV_ROWS, c0:c0 + 128] = acc
    cb_ref[0:CONV_HALO, :] = cb_ref[ts:ts + CONV_HALO, :]

    v = cv_ref[...] + bdw_ref[...]
    mu = jnp.mean(v, axis=-1, keepdims=True)
    vc = v - mu
    var = jnp.mean(vc * vc, axis=-1, keepdims=True)
    y = vc * lax.rsqrt(var + NORM_EPS) * lng_ref[...] + lnb_ref[...]
    y = (y * jax.nn.sigmoid(y)).astype(BF16)
    o_ref[0] = h + _dot(y, wout_ref[...]) + bout_ref[...]


def _conv_mixer(h, g, w_in, b_in, w_dw, b_dw, ln_g, ln_b, w_out, b_out, ts=256):
    b, s, d = h.shape
    row = lambda v: v.reshape(1, -1)
    const = lambda shape: pl.BlockSpec(shape, lambda i, j: (0,) * len(shape))
    return pl.pallas_call(
        _conv_mixer_kernel,
        grid=(b, s // ts),
        in_specs=[
            pl.BlockSpec((1, ts, d), lambda i, j: (i, j, 0)),
            const((1, d)), const((d, 2 * d)), const((1, 2 * d)),
            const((CONV_WIDTH, d)), const((1, d)), const((1, d)), const((1, d)),
            const((d, d)), const((1, d)),
        ],
        out_specs=pl.BlockSpec((1, ts, d), lambda i, j: (i, j, 0)),
        out_shape=jax.ShapeDtypeStruct((b, s, d), F32),
        scratch_shapes=[pltpu.VMEM((ts + CONV_HALO, d), F32), pltpu.VMEM((ts, d), F32)],
        compiler_params=_params("arbitrary", "arbitrary"),
        name="conv_mixer",
    )(h, row(g), w_in.astype(BF16), row(b_in), w_dw, row(b_dw), row(ln_g), row(ln_b),
      w_out.astype(BF16), row(b_out))


def _xattn_kernel(*refs, with_prefix):
    if with_prefix:
        h_ref, a_ref, wmo_ref, g_ref, wq_ref, kv_ref, wo_ref, o_ref = refs
        h = h_ref[0] + _dot(a_ref[0], wmo_ref[...])
    else:
        h_ref, g_ref, wq_ref, kv_ref, wo_ref, o_ref = refs
        h = h_ref[0]
    d = h.shape[-1]
    hd = d // XA_HEADS
    hn = _rms(h, g_ref[...]).astype(BF16)
    q = (_dot(hn, wq_ref[...]) * (1.0 / math.sqrt(hd))).astype(BF16)
    outs = []
    for hh in range(XA_HEADS):
        k = kv_ref[0, 0, :, hh * hd:(hh + 1) * hd]
        v = kv_ref[0, 0, :, d + hh * hd:d + (hh + 1) * hd]
        sc = _dot_nt(q[:, hh * hd:(hh + 1) * hd], k)
        p = jnp.exp(sc - jnp.max(sc, axis=-1, keepdims=True))
        inv = 1.0 / jnp.sum(p, axis=-1, keepdims=True)
        outs.append((_dot(p.astype(BF16), v) * inv).astype(BF16))
    o = jnp.concatenate(outs, axis=-1)
    o_ref[0] = h + _dot(o, wo_ref[...])


def _xattn(h, g, wq, kv, layer, wo, prefix=None, ts=512):
    b, s, d = h.shape
    m = kv.shape[2]
    tile = pl.BlockSpec((1, ts, d), lambda i, j: (i, j, 0))
    const = lambda shape: pl.BlockSpec(shape, lambda i, j: (0,) * len(shape))
    in_specs = [tile]
    args = [h]
    if prefix is not None:
        attn, w_mo = prefix
        in_specs += [tile, const((d, d))]
        args += [attn, w_mo.astype(BF16)]
    in_specs += [const((1, d)), const((d, d)),
                 pl.BlockSpec((1, 1, m, 2 * d), lambda i, j: (layer, i, 0, 0)), const((d, d))]
    args += [g.reshape(1, d), wq.astype(BF16), kv, wo.astype(BF16)]
    return pl.pallas_call(
        functools.partial(_xattn_kernel, with_prefix=prefix is not None),
        grid=(b, s // ts),
        in_specs=in_specs,
        out_specs=tile,
        out_shape=jax.ShapeDtypeStruct((b, s, d), F32),
        compiler_params=_params("arbitrary", "arbitrary"),
        name="xattn",
    )(*args)


def _route(hn, wr_ref, br_ref):
    logits = jnp.dot(hn, wr_ref[...], preferred_element_type=F32, precision=lax.Precision.HIGHEST) + br_ref[...]
    lane = lax.broadcasted_iota(jnp.int32, logits.shape, 1)
    gl = jnp.where(lane < N_GROUPS, logits, NEG_BIG)
    gmax = jnp.max(gl, axis=-1, keepdims=True)
    gsum = jnp.sum(jnp.exp(gl - gmax), axis=-1, keepdims=True)
    g_val = 1.0 / gsum
    g_idx = jnp.min(jnp.where(gl == gmax, lane, ROUTER_LANES), axis=-1, keepdims=True)
    lo = N_GROUPS + g_idx * EXPERTS_PER_GROUP
    el = jnp.where((lane >= lo) & (lane < lo + EXPERTS_PER_GROUP), logits, NEG_BIG)
    m1 = jnp.max(el, axis=-1, keepdims=True)
    esum = jnp.sum(jnp.exp(el - m1), axis=-1, keepdims=True)
    i1 = jnp.min(jnp.where(el == m1, lane, ROUTER_LANES), axis=-1, keepdims=True)
    el2 = jnp.where(lane == i1, NEG_BIG, el)
    m2 = jnp.max(el2, axis=-1, keepdims=True)
    i2 = jnp.min(jnp.where(el2 == m2, lane, ROUTER_LANES), axis=-1, keepdims=True)
    p1 = 1.0 / esum
    p2 = jnp.exp(m2 - m1) / esum
    w1 = p1 / (p1 + p2)
    w2 = p2 / (p1 + p2)
    return g_val * (jnp.where(lane == i1, w1, 0.0) + jnp.where(lane == i2, w2, 0.0))


def _moe_dense_kernel(h_ref, g_ref, wr_ref, br_ref, wg_ref, wu_ref, wd_ref, fin_ref, o_ref,
                      hn_ref, comb_ref, acc_ref, *, final):
    e = pl.program_id(1)

    @pl.when(e == 0)
    def _():
        hn = _rms(h_ref[...], g_ref[...])
        hn_ref[...] = hn.astype(BF16)
        comb_ref[...] = _route(hn, wr_ref, br_ref)
        acc_ref[...] = jnp.zeros_like(acc_ref)

    x = hn_ref[...]
    gate = _dot(x, wg_ref[0])
    he = (gate * jax.nn.sigmoid(gate) * _dot(x, wu_ref[0])).astype(BF16)
    comb = comb_ref[...]
    lane = lax.broadcasted_iota(jnp.int32, comb.shape, 1)
    c = jnp.sum(jnp.where(lane == N_GROUPS + e, comb, 0.0), axis=-1, keepdims=True)
    acc_ref[...] += c * _dot(he, wd_ref[0])

    @pl.when(e == N_EXPERTS - 1)
    def _():
        out = h_ref[...] + acc_ref[...]
        if final:
            out = _rms(out, fin_ref[...])
        o_ref[...] = out


def _moe_dense(h, g, w_grp, b_grp, w_exp, b_exp, w_gate, w_up, w_down, final_norm, final, tm=1024):
    b, s, d = h.shape
    t = b * s
    f = w_gate.shape[-1]
    pad = ROUTER_LANES - N_GROUPS - N_EXPERTS
    wr = jnp.concatenate([w_grp, w_exp, jnp.zeros((d, pad), F32)], axis=1)
    br = jnp.concatenate([b_grp, b_exp, jnp.zeros((pad,), F32)]).reshape(1, ROUTER_LANES)
    const = lambda shape: pl.BlockSpec(shape, lambda i, j: (0,) * len(shape))
    out = pl.pallas_call(
        functools.partial(_moe_dense_kernel, final=final),
        grid=(t // tm, N_EXPERTS),
        in_specs=[
            pl.BlockSpec((tm, d), lambda i, j: (i, 0)),
            const((1, d)), const((d, ROUTER_LANES)), const((1, ROUTER_LANES)),
            pl.BlockSpec((1, d, f), lambda i, j: (j, 0, 0)),
            pl.BlockSpec((1, d, f), lambda i, j: (j, 0, 0)),
            pl.BlockSpec((1, f, d), lambda i, j: (j, 0, 0)),
            const((1, d)),
        ],
        out_specs=pl.BlockSpec((tm, d), lambda i, j: (i, 0)),
        out_shape=jax.ShapeDtypeStruct((t, d), F32),
        scratch_shapes=[pltpu.VMEM((tm, d), BF16), pltpu.VMEM((tm, ROUTER_LANES), F32),
                        pltpu.VMEM((tm, d), F32)],
        compiler_params=_params("arbitrary", "arbitrary"),
        name="moe_dense",
    )(h.reshape(t, d), g.reshape(1, d), wr, br, w_gate.astype(BF16), w_up.astype(BF16),
      w_down.astype(BF16), final_norm.reshape(1, d))
    return out.reshape(b, s, d)


def _mla_proj_kernel(h_ref, g_ref, wdn_ref, qn_ref, wuq_ref, kvn_ref, wukv_ref,
                     tc_ref, ta_ref, tb_ref, q_ref, k_ref, v_ref):
    hn = _rms(h_ref[0], g_ref[...]).astype(BF16)
    dn = _dot(hn, wdn_ref[...])
    c, a, b = tc_ref[...], ta_ref[...], tb_ref[...]
    scale = 1.0 / math.sqrt(QK_NOPE_DIM + QK_ROPE_DIM)
    kv_lo = Q_LORA_RANK
    kr_lo = Q_LORA_RANK + KV_LORA_RANK
    cq = _rms(dn[:, :kv_lo], qn_ref[...]).astype(BF16)
    q = _dot(cq, wuq_ref[...])
    for hh in range(MLA_HEADS):
        sl = slice(hh * HEAD_PAD, (hh + 1) * HEAD_PAD)
        q_ref[0, :, sl] = (_rope_apply(q[:, sl], c, a, b) * scale).astype(BF16)
    ckv = _rms(dn[:, kv_lo:kr_lo], kvn_ref[...]).astype(BF16)
    kv = _dot(ckv, wukv_ref[...])
    kr = _rope_apply(dn[:, kr_lo:kr_lo + HEAD_PAD], c, a, b)
    nk = MLA_HEADS * HEAD_PAD
    for hh in range(MLA_HEADS):
        sl = slice(hh * HEAD_PAD, (hh + 1) * HEAD_PAD)
        k_ref[0, :, sl] = (kv[:, sl] + kr).astype(BF16)
    v_ref[0] = kv[:, nk:].astype(BF16)


def _mla_proj(h, g, w_down, q_norm, w_uq, kv_norm, w_ukv, tabs, ts=256):
    b, s, d = h.shape
    nh, dn_, dr, dv = MLA_HEADS, QK_NOPE_DIM, QK_ROPE_DIM, V_HEAD_DIM
    kr_lo = Q_LORA_RANK + KV_LORA_RANK
    w_dn = jnp.concatenate([
        w_down[:, :kr_lo], jnp.zeros((d, dn_), F32), w_down[:, kr_lo:],
        jnp.zeros((d, HEAD_PAD - dn_ - dr), F32)], axis=1).astype(BF16)
    wq = w_uq.reshape(Q_LORA_RANK, nh, dn_ + dr)
    wq = jnp.pad(wq, ((0, 0), (0, 0), (0, HEAD_PAD - dn_ - dr))).reshape(Q_LORA_RANK, nh * HEAD_PAD).astype(BF16)
    wkv = w_ukv.reshape(KV_LORA_RANK, nh, dn_ + dv)
    wk = jnp.pad(wkv[:, :, :dn_], ((0, 0), (0, 0), (0, HEAD_PAD - dn_))).reshape(KV_LORA_RANK, nh * HEAD_PAD)
    wv = wkv[:, :, dn_:].reshape(KV_LORA_RANK, nh * dv)
    wkv = jnp.concatenate([wk, wv], axis=1).astype(BF16)
    tab_c, tab_a, tab_b = tabs
    const = lambda shape: pl.BlockSpec(shape, lambda i, j: (0,) * len(shape))
    nblk = s // ts
    tab = pl.BlockSpec((ts, HEAD_PAD), lambda i, j: (i * nblk + j, 0))
    tile = lambda w: pl.BlockSpec((1, ts, w), lambda i, j: (i, j, 0))
    return pl.pallas_call(
        _mla_proj_kernel,
        grid=(b, nblk),
        in_specs=[tile(d), const((1, d)), const(w_dn.shape), const((1, Q_LORA_RANK)), const(wq.shape),
                  const((1, KV_LORA_RANK)), const(wkv.shape), tab, tab, tab],
        out_specs=(tile(nh * HEAD_PAD), tile(nh * HEAD_PAD), tile(nh * dv)),
        out_shape=(jax.ShapeDtypeStruct((b, s, nh * HEAD_PAD), BF16),
                   jax.ShapeDtypeStruct((b, s, nh * HEAD_PAD), BF16),
                   jax.ShapeDtypeStruct((b, s, nh * dv), BF16)),
        compiler_params=_params("arbitrary", "arbitrary"),
        name="mla_proj",
    )(h, g.reshape(1, d), w_dn, q_norm.reshape(1, -1), wq, kv_norm.reshape(1, -1), wkv,
      tab_c, tab_a, tab_b)


def _flash_kernel(q_ref, k_ref, v_ref, o_ref, *, tk):
    tq = q_ref.shape[1]
    qi = pl.program_id(2)
    row = lax.broadcasted_iota(jnp.int32, (tq, tk), 0)
    col = lax.broadcasted_iota(jnp.int32, (tq, tk), 1)
    heads = []
    for hh in range(2):
        q = q_ref[0, :, hh * HEAD_PAD:(hh + 1) * HEAD_PAD]

        def tile(j, carry, masked):
            m, l, acc = carry
            start = pl.multiple_of(j * tk, tk)
            k = k_ref[0, pl.ds(start, tk), hh * HEAD_PAD:(hh + 1) * HEAD_PAD]
            sc = _dot_nt(q, k)
            if masked:
                sc = jnp.where(row + qi * tq >= col + j * tk, sc, NEG_BIG)
            m_new = jnp.maximum(m, jnp.max(sc, axis=-1, keepdims=True))
            alpha = jnp.exp(m - m_new)
            p = jnp.exp(sc - m_new)
            l = alpha * l + jnp.sum(p, axis=-1, keepdims=True)
            acc = alpha * acc + _dot(p.astype(BF16), v_ref[0, pl.ds(start, tk), :])
            return m_new, l, acc

        init = (jnp.full((tq, 1), NEG_BIG, F32), jnp.zeros((tq, 1), F32),
                jnp.zeros((tq, 2 * V_HEAD_DIM), F32))
        nfull = qi * (tq // tk)
        carry = lax.fori_loop(0, nfull, functools.partial(tile, masked=False), init)
        for jj in range(tq // tk):
            carry = tile(nfull + jj, carry, masked=True)
        m, l, acc = carry
        heads.append(acc * (1.0 / l))
    lane = lax.broadcasted_iota(jnp.int32, heads[0].shape, 1)
    o_ref[0] = jnp.where(lane < V_HEAD_DIM, heads[0], heads[1]).astype(BF16)


def _flash(q, k, v, tq=512, tk=512):
    b, s, _ = q.shape
    pairs = MLA_HEADS // 2
    return pl.pallas_call(
        functools.partial(_flash_kernel, tk=tk),
        grid=(b, pairs, s // tq),
        in_specs=[
            pl.BlockSpec((1, tq, 2 * HEAD_PAD), lambda i, p, j: (i, j, p)),
            pl.BlockSpec((1, s, 2 * HEAD_PAD), lambda i, p, j: (i, 0, p)),
            pl.BlockSpec((1, s, 2 * V_HEAD_DIM), lambda i, p, j: (i, 0, p)),
        ],
        out_specs=pl.BlockSpec((1, tq, 2 * V_HEAD_DIM), lambda i, p, j: (i, j, p)),
        out_shape=jax.ShapeDtypeStruct((b, s, MLA_HEADS * V_HEAD_DIM), BF16),
        compiler_params=_params("arbitrary", "arbitrary", "arbitrary"),
        name="flash",
    )(q, k, v)


def kernel(x, mem, positions, norm_mix, conv_w_in, conv_b_in, conv_w_dw, conv_b_dw, conv_ln_g, conv_ln_b, conv_w_out, conv_b_out, mla_w_down, mla_q_norm, mla_w_uq, mla_kv_norm, mla_w_ukv, mla_w_o, norm_xa, norm_mem, xa_wq, xa_wkv, xa_wo, norm_ffn, moe_w_grp, moe_b_grp, moe_w_exp, moe_b_exp, moe_w_gate, moe_w_up, moe_w_down, final_norm):
    tabs = _rope_tables(positions)
    kv_mem = _mem_kv(mem, norm_mem, xa_wkv.astype(BF16))

    def ffn(h, i, final):
        return _moe_dense(h, norm_ffn[i], moe_w_grp[i], moe_b_grp[i], moe_w_exp[i], moe_b_exp[i],
                          moe_w_gate[i], moe_w_up[i], moe_w_down[i], final_norm, final)

    h = _conv_mixer(x, norm_mix[0], conv_w_in[0], conv_b_in[0], conv_w_dw[0], conv_b_dw[0],
                    conv_ln_g[0], conv_ln_b[0], conv_w_out[0], conv_b_out[0])
    h = _xattn(h, norm_xa[0], xa_wq[0], kv_mem, 0, xa_wo[0])
    h = ffn(h, 0, False)
    q, k, v = _mla_proj(h, norm_mix[1], mla_w_down[0], mla_q_norm[0], mla_w_uq[0], mla_kv_norm[0],
                        mla_w_ukv[0], tabs)
    attn = _flash(q, k, v)
    h = _xattn(h, norm_xa[1], xa_wq[1], kv_mem, 1, xa_wo[1], prefix=(attn, mla_w_o[0]))
    return ffn(h, 1, True)
```

```python
import functools
import math

import jax
import jax.numpy as jnp
from jax import lax
from jax.experimental import pallas as pl
from jax.experimental.pallas import tpu as pltpu

F32 = jnp.float32
BF16 = jnp.bfloat16
I32 = jnp.int32

NORM_EPS = 1e-6
CONV_WIDTH = 31
CONV_HALO = 32
SUBLANES = 8
MLA_HEADS = 16
QK_NOPE_DIM = 64
QK_ROPE_DIM = 32
V_HEAD_DIM = 64
Q_LORA_RANK = 384
KV_LORA_RANK = 256
ROPE_THETA = 10000.0
HEAD_PAD = 128
XA_HEADS = 4
N_GROUPS = 4
EXPERTS_PER_GROUP = 8
N_EXPERTS = N_GROUPS * EXPERTS_PER_GROUP
TOP_K = 2
ROUTER_LANES = 128
NEG_BIG = -1e30
LOG2_E = 1.4426950408889634

VMEM_LIMIT = 56 * 1024 * 1024


def _rms(x, g):
    return x * lax.rsqrt(jnp.mean(x * x, axis=-1, keepdims=True) + NORM_EPS) * g


def _dot(a, b):
    return jnp.dot(a, b, preferred_element_type=F32)


def _dot_nt(a, b):
    return lax.dot_general(a, b, (((1,), (1,)), ((), ())), preferred_element_type=F32)


def _params(*sem):
    return pltpu.CompilerParams(dimension_semantics=sem, vmem_limit_bytes=VMEM_LIMIT)


def _const_spec(shape):
    return pl.BlockSpec(shape, lambda *_: (0,) * len(shape))


def _rope_kernel(pos_ref, freq_ref, cos_ref, sin_ref):
    ang = pos_ref[...].astype(F32) * freq_ref[...]
    cos_ref[...] = jnp.cos(ang)
    sin_ref[...] = jnp.sin(ang)


def _rope_tables(positions):
    half = QK_ROPE_DIM // 2
    t = positions.size
    inv_freq = 1.0 / (ROPE_THETA ** (jnp.arange(0, QK_ROPE_DIM, 2, dtype=F32) / QK_ROPE_DIM))
    rows = t * half // 128
    pos_rep = jnp.repeat(positions.reshape(-1), half).reshape(rows, 128)
    freq = jnp.tile(inv_freq, 128 // half).reshape(1, 128)
    cos, sin = pl.pallas_call(
        _rope_kernel,
        out_shape=(jax.ShapeDtypeStruct((rows, 128), F32),) * 2,
        name="rope_tables",
    )(pos_rep, freq)
    cos = cos.reshape(t, half)
    sin = sin.reshape(t, half)
    one = jnp.ones((t, QK_NOPE_DIM), F32)
    z16 = jnp.zeros((t, half), F32)
    z64 = jnp.zeros((t, QK_NOPE_DIM), F32)
    pad1 = jnp.ones((t, HEAD_PAD - QK_NOPE_DIM - QK_ROPE_DIM), F32)
    pad0 = jnp.zeros((t, HEAD_PAD - QK_NOPE_DIM - QK_ROPE_DIM), F32)
    tab_c = jnp.concatenate([one, cos, cos, pad1], axis=1)
    tab_a = jnp.concatenate([z64, sin, z16, pad0], axis=1)
    tab_b = jnp.concatenate([z64, z16, sin, pad0], axis=1)
    return tab_c, tab_a, tab_b


def _rope_apply(x, c, a, b):
    return x * c - pltpu.roll(x, HEAD_PAD - 16, 1) * a + pltpu.roll(x, 16, 1) * b


def _mem_kv_kernel(mem_ref, g_ref, w_ref, o_ref):
    mn = _rms(mem_ref[0], g_ref[0]).astype(BF16)
    o_ref[0, 0] = _dot(mn, w_ref[0]).astype(BF16)


def _mem_kv(mem, norm_mem, wkv):
    b, m, d = mem.shape
    l = norm_mem.shape[0]
    return pl.pallas_call(
        _mem_kv_kernel,
        grid=(l, b),
        in_specs=[
            pl.BlockSpec((1, m, d), lambda i, j: (j, 0, 0)),
            pl.BlockSpec((1, 1, d), lambda i, j: (i, 0, 0)),
            pl.BlockSpec((1, d, 2 * d), lambda i, j: (i, 0, 0)),
        ],
        out_specs=pl.BlockSpec((1, 1, m, 2 * d), lambda i, j: (i, j, 0, 0)),
        out_shape=jax.ShapeDtypeStruct((l, b, m, 2 * d), BF16),
        compiler_params=_params("arbitrary", "arbitrary"),
        name="mem_kv",
    )(mem, norm_mem.reshape(l, 1, d), wkv)


CONV_ROWS = 64


def _conv_mixer_kernel(h_ref, g_ref, win_ref, bin_ref, wdw_ref, bdw_ref, lng_ref, lnb_ref,
                       wout_ref, bout_ref, o_ref, cb_ref, cv_ref):
    ts, d = h_ref.shape[1], h_ref.shape[2]

    @pl.when(pl.program_id(1) == 0)
    def _():
        cb_ref[0, 0:CONV_HALO, :] = jnp.zeros((CONV_HALO, d), F32)

    h = h_ref[0]
    hn = _rms(h, g_ref[...]).astype(BF16)
    u = _dot(hn, win_ref[...]) + bin_ref[...]
    cb_ref[0, CONV_HALO:CONV_HALO + ts, :] = u[:, :d] * jax.nn.sigmoid(u[:, d:])
    ncopy = ts + CONV_HALO - SUBLANES
    for s in range(1, SUBLANES):
        cb_ref[s, 0:ncopy, :] = cb_ref[0, s:s + ncopy, :]

    base = CONV_HALO - (CONV_WIDTH - 1)

    def rows(i, carry):
        r0 = pl.multiple_of(i * CONV_ROWS, CONV_ROWS)
        for c0 in range(0, d, 128):
            acc = jnp.zeros((CONV_ROWS, 128), F32)
            for k in range(CONV_WIDTH):
                off = base + k
                row0 = pl.multiple_of(r0 + (off // SUBLANES) * SUBLANES, SUBLANES)
                tap = cb_ref[off % SUBLANES, pl.ds(row0, CONV_ROWS), c0:c0 + 128]
                acc = acc + tap * wdw_ref[k:k + 1, c0:c0 + 128]
            cv_ref[pl.ds(r0, CONV_ROWS), c0:c0 + 128] = acc
        return carry

    lax.fori_loop(0, ts // CONV_ROWS, rows, 0)
    cb_ref[0, 0:CONV_HALO, :] = cb_ref[0, ts:ts + CONV_HALO, :]

    v = cv_ref[...] + bdw_ref[...]
    mu = jnp.mean(v, axis=-1, keepdims=True)
    vc = v - mu
    var = jnp.mean(vc * vc, axis=-1, keepdims=True)
    y = vc * lax.rsqrt(var + NORM_EPS) * lng_ref[...] + lnb_ref[...]
    y = (y * jax.nn.sigmoid(y)).astype(BF16)
    o_ref[0] = h + _dot(y, wout_ref[...]) + bout_ref[...]


def _conv_mixer(h, g, w_in, b_in, w_dw, b_dw, ln_g, ln_b, w_out, b_out, ts=256):
    b, s, d = h.shape
    row = lambda v: v.reshape(1, -1)
    return pl.pallas_call(
        _conv_mixer_kernel,
        grid=(b, s // ts),
        in_specs=[
            pl.BlockSpec((1, ts, d), lambda i, j: (i, j, 0)),
            _const_spec((1, d)), _const_spec((d, 2 * d)), _const_spec((1, 2 * d)),
            _const_spec((CONV_WIDTH, d)), _const_spec((1, d)), _const_spec((1, d)), _const_spec((1, d)),
            _const_spec((d, d)), _const_spec((1, d)),
        ],
        out_specs=pl.BlockSpec((1, ts, d), lambda i, j: (i, j, 0)),
        out_shape=jax.ShapeDtypeStruct((b, s, d), F32),
        scratch_shapes=[pltpu.VMEM((SUBLANES, ts + CONV_HALO, d), F32), pltpu.VMEM((ts, d), F32)],
        compiler_params=_params("arbitrary", "arbitrary"),
        name="conv_mixer",
    )(h, row(g), w_in.astype(BF16), row(b_in), w_dw, row(b_dw), row(ln_g), row(ln_b),
      w_out.astype(BF16), row(b_out))


def _xattn_kernel(*refs, with_prefix):
    if with_prefix:
        h_ref, a_ref, wmo_ref, g_ref, wq_ref, kv_ref, wo_ref, o_ref = refs
        h = h_ref[0] + _dot(a_ref[0], wmo_ref[...])
    else:
        h_ref, g_ref, wq_ref, kv_ref, wo_ref, o_ref = refs
        h = h_ref[0]
    d = h.shape[-1]
    hd = d // XA_HEADS
    hn = _rms(h, g_ref[...]).astype(BF16)
    q = (_dot(hn, wq_ref[...]) * (1.0 / math.sqrt(hd))).astype(BF16)
    outs = []
    for hh in range(XA_HEADS):
        k = kv_ref[0, 0, :, hh * hd:(hh + 1) * hd]
        v = kv_ref[0, 0, :, d + hh * hd:d + (hh + 1) * hd]
        sc = _dot_nt(q[:, hh * hd:(hh + 1) * hd], k)
        p = jnp.exp(sc - jnp.max(sc, axis=-1, keepdims=True))
        inv = 1.0 / jnp.sum(p, axis=-1, keepdims=True)
        outs.append((_dot(p.astype(BF16), v) * inv).astype(BF16))
    o = jnp.concatenate(outs, axis=-1)
    o_ref[0] = h + _dot(o, wo_ref[...])


def _xattn(h, g, wq, kv, layer, wo, prefix=None, ts=512):
    b, s, d = h.shape
    m = kv.shape[2]
    tile = pl.BlockSpec((1, ts, d), lambda i, j: (i, j, 0))
    in_specs = [tile]
    args = [h]
    if prefix is not None:
        attn, w_mo = prefix
        in_specs += [tile, _const_spec((d, d))]
        args += [attn, w_mo.astype(BF16)]
    in_specs += [_const_spec((1, d)), _const_spec((d, d)),
                 pl.BlockSpec((1, 1, m, 2 * d), lambda i, j: (layer, i, 0, 0)), _const_spec((d, d))]
    args += [g.reshape(1, d), wq.astype(BF16), kv, wo.astype(BF16)]
    return pl.pallas_call(
        functools.partial(_xattn_kernel, with_prefix=prefix is not None),
        grid=(b, s // ts),
        in_specs=in_specs,
        out_specs=tile,
        out_shape=jax.ShapeDtypeStruct((b, s, d), F32),
        compiler_params=_params("arbitrary", "arbitrary"),
        name="xattn",
    )(*args)


ROUTE_TILE = 512
EXPERT_TILE = 256
COMBINE_TILE = 256
DISPATCH_CHUNK = 256
INFO_E1, INFO_E2, INFO_R1, INFO_R2, INFO_W1, INFO_W2 = range(6)


def _router_kernel(h_ref, g_ref, wr_ref, br_ref, hn_ref, info_ref, cnt_ref, carry_ref):
    tm = h_ref.shape[0]

    @pl.when(pl.program_id(0) == 0)
    def _():
        carry_ref[...] = jnp.zeros_like(carry_ref)

    hn = _rms(h_ref[...], g_ref[...])
    hn_ref[...] = hn
    logits = jnp.dot(hn, wr_ref[...], preferred_element_type=F32, precision=lax.Precision.HIGHEST) + br_ref[...]
    lane = lax.broadcasted_iota(I32, logits.shape, 1)
    gl = jnp.where(lane < N_GROUPS, logits, NEG_BIG)
    gmax = jnp.max(gl, axis=-1, keepdims=True)
    gsum = jnp.sum(jnp.exp(gl - gmax), axis=-1, keepdims=True)
    g_val = 1.0 / gsum
    g_idx = jnp.min(jnp.where(gl == gmax, lane, ROUTER_LANES), axis=-1, keepdims=True)
    lo = N_GROUPS + g_idx * EXPERTS_PER_GROUP
    el = jnp.where((lane >= lo) & (lane < lo + EXPERTS_PER_GROUP), logits, NEG_BIG)
    m1 = jnp.max(el, axis=-1, keepdims=True)
    esum = jnp.sum(jnp.exp(el - m1), axis=-1, keepdims=True)
    i1 = jnp.min(jnp.where(el == m1, lane, ROUTER_LANES), axis=-1, keepdims=True)
    el2 = jnp.where(lane == i1, NEG_BIG, el)
    m2 = jnp.max(el2, axis=-1, keepdims=True)
    i2 = jnp.min(jnp.where(el2 == m2, lane, ROUTER_LANES), axis=-1, keepdims=True)
    p1 = 1.0 / esum
    p2 = jnp.exp(m2 - m1) / esum
    w1 = g_val * (p1 / (p1 + p2))
    w2 = g_val * (p2 / (p1 + p2))
    hit = (lane == i1) | (lane == i2)
    r = lax.broadcasted_iota(I32, (tm, tm), 0)
    c = lax.broadcasted_iota(I32, (tm, tm), 1)
    ltri = jnp.where(c < r, 1.0, 0.0).astype(BF16)
    tot = _dot(ltri, jnp.where(hit, 1.0, 0.0).astype(BF16)) + carry_ref[0:1, :]
    rank1 = jnp.sum(jnp.where(lane == i1, tot, 0.0), axis=-1, keepdims=True)
    rank2 = jnp.sum(jnp.where(lane == i2, tot, 0.0), axis=-1, keepdims=True)
    carry_ref[0:1, :] = carry_ref[0:1, :] + jnp.sum(jnp.where(hit, 1.0, 0.0), axis=0, keepdims=True)
    cnt_ref[...] = jnp.broadcast_to(carry_ref[0:1, :], cnt_ref.shape)
    info = jnp.zeros(logits.shape, F32)
    for ln, val in ((INFO_E1, (i1 - N_GROUPS).astype(F32)), (INFO_E2, (i2 - N_GROUPS).astype(F32)),
                    (INFO_R1, rank1), (INFO_R2, rank2), (INFO_W1, w1), (INFO_W2, w2)):
        info = jnp.where(lane == ln, val, info)
    info_ref[...] = info


def _dispatch_kernel(p1_ref, p2_ref, hn_ref, xs_ref, sem):
    t = hn_ref.shape[0]

    def wait_chunk():
        cp = pltpu.make_async_copy(hn_ref.at[pl.ds(0, DISPATCH_CHUNK)], xs_ref.at[pl.ds(0, DISPATCH_CHUNK)], sem)
        for _ in range(TOP_K):
            cp.wait()

    def chunk(c, carry):
        def tok(r, carry2):
            tt = c * DISPATCH_CHUNK + r
            src = hn_ref.at[pl.ds(tt, 1)]
            pltpu.make_async_copy(src, xs_ref.at[pl.ds(p1_ref[tt], 1)], sem).start()
            pltpu.make_async_copy(src, xs_ref.at[pl.ds(p2_ref[tt], 1)], sem).start()
            return carry2

        lax.fori_loop(0, DISPATCH_CHUNK, tok, 0, unroll=8)

        @pl.when(c > 0)
        def _():
            wait_chunk()

        return carry

    lax.fori_loop(0, t // DISPATCH_CHUNK, chunk, 0)
    wait_chunk()


def _expert_kernel(ve_ref, vt_ref, lo_ref, hi_ref, nv_ref, x_ref, wg_ref, wu_ref, wd_ref, y_ref, wgu_s, wd_s):
    v = pl.program_id(0)
    f = wg_ref.shape[2]

    @pl.when(v < nv_ref[0])
    def _():
        pv = jnp.maximum(v - 1, 0)

        @pl.when((v == 0) | (ve_ref[v] != ve_ref[pv]))
        def _():
            wgu_s[:, :f] = wg_ref[0].astype(BF16)
            wgu_s[:, f:] = wu_ref[0].astype(BF16)
            wd_s[...] = wd_ref[0].astype(BF16)

        gu = _dot(x_ref[...].astype(BF16), wgu_s[...])
        gate = gu[:, :f]
        he = (gate * jax.nn.sigmoid(gate) * gu[:, f:]).astype(BF16)
        row = lax.broadcasted_iota(I32, (x_ref.shape[0], 1), 0)
        y = jnp.where((row >= lo_ref[v]) & (row < hi_ref[v]), _dot(he, wd_s[...]), 0.0)
        first = (v == 0) | (vt_ref[v] != vt_ref[pv])

        @pl.when(first)
        def _():
            y_ref[...] = y

        @pl.when(jnp.logical_not(first))
        def _():
            y_ref[...] += y


def _combine_kernel(p1_ref, p2_ref, h_ref, info_ref, fin_ref, ys_ref, o_ref, buf, sem, *, final):
    i = pl.program_id(0)
    n = pl.num_programs(0)
    tm = h_ref.shape[0]
    slot = i % 2

    def issue(step, sl):
        def tok(r, carry):
            tt = step * tm + r
            pltpu.make_async_copy(ys_ref.at[pl.ds(p1_ref[tt], 1)], buf.at[sl, 0, pl.ds(r, 1)], sem.at[sl]).start()
            pltpu.make_async_copy(ys_ref.at[pl.ds(p2_ref[tt], 1)], buf.at[sl, 1, pl.ds(r, 1)], sem.at[sl]).start()
            return carry

        lax.fori_loop(0, tm, tok, 0, unroll=8)

    @pl.when(i == 0)
    def _():
        issue(0, 0)

    @pl.when(i + 1 < n)
    def _():
        issue(i + 1, 1 - slot)

    for k in range(TOP_K):
        pltpu.make_async_copy(ys_ref.at[pl.ds(0, tm)], buf.at[slot, k], sem.at[slot]).wait()
    info = info_ref[...]
    out = (h_ref[...] + info[:, INFO_W1:INFO_W1 + 1] * buf[slot, 0]
           + info[:, INFO_W2:INFO_W2 + 1] * buf[slot, 1])
    if final:
        out = _rms(out, fin_ref[...])
    o_ref[...] = out


def _moe(h, g, w_grp, b_grp, w_exp, b_exp, w_gate, w_up, w_down, final_norm, final):
    b, s, d = h.shape
    t = b * s
    f = w_gate.shape[-1]
    h2 = h.reshape(t, d)
    pad = ROUTER_LANES - N_GROUPS - N_EXPERTS
    wr = jnp.concatenate([w_grp, w_exp, jnp.zeros((d, pad), F32)], axis=1)
    br = jnp.concatenate([b_grp, b_exp, jnp.zeros((pad,), F32)]).reshape(1, ROUTER_LANES)

    hn, info, cnt = pl.pallas_call(
        _router_kernel,
        grid=(t // ROUTE_TILE,),
        in_specs=[pl.BlockSpec((ROUTE_TILE, d), lambda i: (i, 0)), _const_spec((1, d)),
                  _const_spec((d, ROUTER_LANES)), _const_spec((1, ROUTER_LANES))],
        out_specs=(pl.BlockSpec((ROUTE_TILE, d), lambda i: (i, 0)),
                   pl.BlockSpec((ROUTE_TILE, ROUTER_LANES), lambda i: (i, 0)),
                   _const_spec((SUBLANES, ROUTER_LANES))),
        out_shape=(jax.ShapeDtypeStruct((t, d), F32), jax.ShapeDtypeStruct((t, ROUTER_LANES), F32),
                   jax.ShapeDtypeStruct((SUBLANES, ROUTER_LANES), F32)),
        scratch_shapes=[pltpu.VMEM((SUBLANES, ROUTER_LANES), F32)],
        compiler_params=_params("arbitrary"),
        name="moe_router",
    )(h2, g.reshape(1, d), wr, br)

    tm = EXPERT_TILE
    n_rows = TOP_K * t
    n_visits = n_rows // tm + N_EXPERTS - 1
    counts = cnt[0, N_GROUPS:N_GROUPS + N_EXPERTS].astype(I32)
    row_end = jnp.cumsum(counts)
    row_start = row_end - counts
    first_tile = row_start // tm
    e_visits = jnp.where(counts > 0, (row_end - 1) // tm - first_tile + 1, 0)
    v_end = jnp.cumsum(e_visits)
    v_start = v_end - e_visits
    n_v = v_end[-1:]
    vis = jnp.minimum(jnp.arange(n_visits, dtype=I32), n_v - 1)
    v_e = jnp.sum(vis[:, None] >= v_end[None, :], axis=1).astype(I32)
    v_tile = (first_tile[v_e] + vis - v_start[v_e]).astype(I32)
    v_lo = (jnp.maximum(row_start[v_e], v_tile * tm) - v_tile * tm).astype(I32)
    v_hi = (jnp.minimum(row_end[v_e], (v_tile + 1) * tm) - v_tile * tm).astype(I32)
    e1 = info[:, INFO_E1].astype(I32)
    e2 = info[:, INFO_E2].astype(I32)
    pos1 = row_start[e1] + info[:, INFO_R1].astype(I32)
    pos2 = row_start[e2] + info[:, INFO_R2].astype(I32)

    xs = pl.pallas_call(
        _dispatch_kernel,
        grid_spec=pltpu.PrefetchScalarGridSpec(
            num_scalar_prefetch=2, grid=(1,),
            in_specs=[pl.BlockSpec(memory_space=pl.ANY)],
            out_specs=pl.BlockSpec(memory_space=pl.ANY),
            scratch_shapes=[pltpu.SemaphoreType.DMA],
        ),
        out_shape=jax.ShapeDtypeStruct((n_rows, d), F32),
        compiler_params=_params("arbitrary"),
        name="moe_dispatch",
    )(pos1, pos2, hn)

    by_tile = lambda v, ve, vt, lo, hi, nv: (vt[v], 0)
    by_expert = lambda v, ve, vt, lo, hi, nv: (ve[v], 0, 0)
    ys = pl.pallas_call(
        _expert_kernel,
        grid_spec=pltpu.PrefetchScalarGridSpec(
            num_scalar_prefetch=5, grid=(n_visits,),
            in_specs=[
                pl.BlockSpec((tm, d), by_tile),
                pl.BlockSpec((1, d, f), by_expert),
                pl.BlockSpec((1, d, f), by_expert),
                pl.BlockSpec((1, f, d), by_expert),
            ],
            out_specs=pl.BlockSpec((tm, d), by_tile),
            scratch_shapes=[pltpu.VMEM((d, 2 * f), BF16), pltpu.VMEM((f, d), BF16)],
        ),
        out_shape=jax.ShapeDtypeStruct((n_rows, d), F32),
        compiler_params=_params("arbitrary"),
        name="moe_experts",
    )(v_e, v_tile, v_lo, v_hi, n_v, xs, w_gate, w_up, w_down)

    tc = COMBINE_TILE
    out = pl.pallas_call(
        functools.partial(_combine_kernel, final=final),
        grid_spec=pltpu.PrefetchScalarGridSpec(
            num_scalar_prefetch=2, grid=(t // tc,),
            in_specs=[
                pl.BlockSpec((tc, d), lambda i, p1, p2: (i, 0)),
                pl.BlockSpec((tc, ROUTER_LANES), lambda i, p1, p2: (i, 0)),
                pl.BlockSpec((1, d), lambda i, p1, p2: (0, 0)),
                pl.BlockSpec(memory_space=pl.ANY),
            ],
            out_specs=pl.BlockSpec((tc, d), lambda i, p1, p2: (i, 0)),
            scratch_shapes=[pltpu.VMEM((2, TOP_K, tc, d), F32), pltpu.SemaphoreType.DMA((2,))],
        ),
        out_shape=jax.ShapeDtypeStruct((t, d), F32),
        compiler_params=_params("arbitrary"),
        name="moe_combine",
    )(pos1, pos2, h2, info, final_norm.reshape(1, d), ys)
    return out.reshape(b, s, d)


MLA_TILE = 256
VT_ROWS = V_HEAD_DIM + 16


def _mla_proj_kernel(h_ref, g_ref, wdn_ref, qn_ref, wuq_ref, kvn_ref, wuk_ref, wuvt_ref,
                     tc_ref, ta_ref, tb_ref, q_ref, k_ref, vt_ref):
    hn = _rms(h_ref[0], g_ref[...]).astype(BF16)
    dn = _dot(hn, wdn_ref[...])
    c, a, b = tc_ref[...], ta_ref[...], tb_ref[...]
    scale = LOG2_E / math.sqrt(QK_NOPE_DIM + QK_ROPE_DIM)
    kv_lo = Q_LORA_RANK
    kr_lo = Q_LORA_RANK + KV_LORA_RANK
    cq = _rms(dn[:, :kv_lo], qn_ref[...]).astype(BF16)
    q = _dot(cq, wuq_ref[...])
    for hh in range(MLA_HEADS):
        sl = slice(hh * HEAD_PAD, (hh + 1) * HEAD_PAD)
        q_ref[0, :, sl] = (_rope_apply(q[:, sl], c, a, b) * scale).astype(BF16)
    ckv = _rms(dn[:, kv_lo:kr_lo], kvn_ref[...]).astype(BF16)
    kn = _dot(ckv, wuk_ref[...])
    kr = _rope_apply(dn[:, kr_lo:kr_lo + HEAD_PAD], c, a, b)
    for hh in range(MLA_HEADS):
        sl = slice(hh * HEAD_PAD, (hh + 1) * HEAD_PAD)
        k_ref[0, :, sl] = (kn[:, sl] + kr).astype(BF16)
    vt = _dot_nt(wuvt_ref[...], ckv).astype(BF16)
    ones = jnp.ones((VT_ROWS - V_HEAD_DIM, vt.shape[1]), BF16)
    for hh in range(MLA_HEADS):
        vt_ref[0, hh, 0, 0:V_HEAD_DIM, :] = vt[hh * V_HEAD_DIM:(hh + 1) * V_HEAD_DIM, :]
        vt_ref[0, hh, 0, V_HEAD_DIM:VT_ROWS, :] = ones


def _mla_proj(h, g, w_down, q_norm, w_uq, kv_norm, w_ukv, tabs):
    b, s, d = h.shape
    ts = MLA_TILE
    nh, dn_, dr, dv = MLA_HEADS, QK_NOPE_DIM, QK_ROPE_DIM, V_HEAD_DIM
    kr_lo = Q_LORA_RANK + KV_LORA_RANK
    w_dn = jnp.concatenate([
        w_down[:, :kr_lo], jnp.zeros((d, dn_), F32), w_down[:, kr_lo:],
        jnp.zeros((d, HEAD_PAD - dn_ - dr), F32)], axis=1).astype(BF16)
    wq = w_uq.reshape(Q_LORA_RANK, nh, dn_ + dr)
    wq = jnp.pad(wq, ((0, 0), (0, 0), (0, HEAD_PAD - dn_ - dr))).reshape(Q_LORA_RANK, nh * HEAD_PAD).astype(BF16)
    wkv = w_ukv.reshape(KV_LORA_RANK, nh, dn_ + dv)
    wk = jnp.pad(wkv[:, :, :dn_], ((0, 0), (0, 0), (0, HEAD_PAD - dn_))).reshape(KV_LORA_RANK, nh * HEAD_PAD).astype(BF16)
    wvt = wkv[:, :, dn_:].reshape(KV_LORA_RANK, nh * dv).T.astype(BF16)
    tab_c, tab_a, tab_b = tabs
    nblk = s // ts
    tab = pl.BlockSpec((ts, HEAD_PAD), lambda i, j: (i * nblk + j, 0))
    tile = lambda w: pl.BlockSpec((1, ts, w), lambda i, j: (i, j, 0))
    return pl.pallas_call(
        _mla_proj_kernel,
        grid=(b, nblk),
        in_specs=[tile(d), _const_spec((1, d)), _const_spec(w_dn.shape), _const_spec((1, Q_LORA_RANK)),
                  _const_spec(wq.shape), _const_spec((1, KV_LORA_RANK)), _const_spec(wk.shape),
                  _const_spec(wvt.shape), tab, tab, tab],
        out_specs=(tile(nh * HEAD_PAD), tile(nh * HEAD_PAD),
                   pl.BlockSpec((1, nh, 1, VT_ROWS, ts), lambda i, j: (i, 0, j, 0, 0))),
        out_shape=(jax.ShapeDtypeStruct((b, s, nh * HEAD_PAD), BF16),
                   jax.ShapeDtypeStruct((b, s, nh * HEAD_PAD), BF16),
                   jax.ShapeDtypeStruct((b, nh, nblk, VT_ROWS, ts), BF16)),
        compiler_params=_params("arbitrary", "arbitrary"),
        name="mla_proj",
    )(h, g.reshape(1, d), w_dn, q_norm.reshape(1, -1), wq, kv_norm.reshape(1, -1), wk, wvt,
      tab_c, tab_a, tab_b)


def _flash_kernel(q_ref, k_ref, vt_ref, o_ref, sa_ref, sb_ref, acc_ref):
    tq = q_ref.shape[1]
    tk = vt_ref.shape[4]
    qi = pl.program_id(2)

    def scores(j, s_ref, c0=0):
        k2 = k_ref[0, pl.ds(pl.multiple_of(j * tk, tk), tk), :]
        for hh in range(2):
            sl = slice(hh * HEAD_PAD, (hh + 1) * HEAD_PAD)
            s_ref[hh, :, c0:] = _dot_nt(k2[:, sl], q_ref[0, c0:, sl])

    def consume(j, s_ref, ms, c0=0, diag=False):
        out = []
        for hh in range(2):
            st = s_ref[hh, :, c0:]
            if diag:
                key = lax.broadcasted_iota(I32, st.shape, 0)
                qry = lax.broadcasted_iota(I32, st.shape, 1)
                st = jnp.where(key <= qry, st, NEG_BIG)
            m = ms[hh][:, c0:]
            m_new = jnp.maximum(m, jnp.max(st, axis=0, keepdims=True))
            alpha = jnp.exp2(m - m_new)
            p = jnp.exp2(st - m_new).astype(BF16)
            acc_ref[hh, :, c0:] = alpha * acc_ref[hh, :, c0:] + _dot(vt_ref[0, hh, j], p)
            out.append(m_new if c0 == 0 else jnp.concatenate([ms[hh][:, :c0], m_new], axis=1))
        return tuple(out)

    acc_ref[...] = jnp.zeros_like(acc_ref)
    ms = tuple(jnp.full((1, tq), NEG_BIG, F32) for _ in range(2))
    scores(0, sa_ref)

    def two_tiles(i, ms):
        scores(2 * i + 1, sb_ref)
        ms = consume(2 * i, sa_ref, ms)
        scores(2 * i + 2, sa_ref)
        return consume(2 * i + 1, sb_ref, ms)

    assert tq == 2 * tk
    ms = lax.fori_loop(0, qi, two_tiles, ms)
    scores(2 * qi + 1, sb_ref, c0=tk)
    ms = consume(2 * qi, sa_ref, ms, diag=True)
    consume(2 * qi + 1, sb_ref, ms, c0=tk, diag=True)
    heads = [acc_ref[hh, 0:V_HEAD_DIM, :] * (1.0 / acc_ref[hh, V_HEAD_DIM:V_HEAD_DIM + 1, :]) for hh in range(2)]
    o_ref[0] = jnp.concatenate(heads, axis=0).T.astype(BF16)


def _flash(q, k, vt, tq=512):
    b, s, _ = q.shape
    _, nh, nblk, rows, tk = vt.shape
    return pl.pallas_call(
        _flash_kernel,
        grid=(b, nh // 2, s // tq),
        in_specs=[
            pl.BlockSpec((1, tq, 2 * HEAD_PAD), lambda i, p, j: (i, j, p)),
            pl.BlockSpec((1, s, 2 * HEAD_PAD), lambda i, p, j: (i, 0, p)),
            pl.BlockSpec((1, 2, nblk, rows, tk), lambda i, p, j: (i, p, 0, 0, 0)),
        ],
        out_specs=pl.BlockSpec((1, tq, 2 * V_HEAD_DIM), lambda i, p, j: (i, j, p)),
        out_shape=jax.ShapeDtypeStruct((b, s, nh * V_HEAD_DIM), BF16),
        scratch_shapes=[pltpu.VMEM((2, tk, tq), F32), pltpu.VMEM((2, tk, tq), F32),
                        pltpu.VMEM((2, rows, tq), F32)],
        compiler_params=_params("arbitrary", "arbitrary", "arbitrary"),
        name="flash",
    )(q, k, vt)


def kernel(x, mem, positions, norm_mix, conv_w_in, conv_b_in, conv_w_dw, conv_b_dw, conv_ln_g, conv_ln_b, conv_w_out, conv_b_out, mla_w_down, mla_q_norm, mla_w_uq, mla_kv_norm, mla_w_ukv, mla_w_o, norm_xa, norm_mem, xa_wq, xa_wkv, xa_wo, norm_ffn, moe_w_grp, moe_b_grp, moe_w_exp, moe_b_exp, moe_w_gate, moe_w_up, moe_w_down, final_norm):
    tabs = _rope_tables(positions)
    kv_mem = _mem_kv(mem, norm_mem, xa_wkv.astype(BF16))

    def ffn(h, i, final):
        return _moe(h, norm_ffn[i], moe_w_grp[i], moe_b_grp[i], moe_w_exp[i], moe_b_exp[i],
                    moe_w_gate[i], moe_w_up[i], moe_w_down[i], final_norm, final)

    h = _conv_mixer(x, norm_mix[0], conv_w_in[0], conv_b_in[0], conv_w_dw[0], conv_b_dw[0],
                    conv_ln_g[0], conv_ln_b[0], conv_w_out[0], conv_b_out[0])
    h = _xattn(h, norm_xa[0], xa_wq[0], kv_mem, 0, xa_wo[0])
    h = ffn(h, 0, False)
    q, k, vt = _mla_proj(h, norm_mix[1], mla_w_down[0], mla_q_norm[0], mla_w_uq[0], mla_kv_norm[0],
                         mla_w_ukv[0], tabs)
    attn = _flash(q, k, vt)
    h = _xattn(h, norm_xa[1], xa_wq[1], kv_mem, 1, xa_wo[1], prefix=(attn, mla_w_o[0]))
    return ffn(h, 1, True)
```

```python
import functools
import math

import jax
import jax.numpy as jnp
from jax import lax
from jax.experimental import pallas as pl
from jax.experimental.pallas import tpu as pltpu

F32 = jnp.float32
BF16 = jnp.bfloat16
I32 = jnp.int32

NORM_EPS = 1e-6
CONV_WIDTH = 31
CONV_HALO = 32
SUBLANES = 8
MLA_HEADS = 16
QK_NOPE_DIM = 64
QK_ROPE_DIM = 32
V_HEAD_DIM = 64
Q_LORA_RANK = 384
KV_LORA_RANK = 256
ROPE_THETA = 10000.0
HEAD_PAD = 128
XA_HEADS = 4
N_GROUPS = 4
EXPERTS_PER_GROUP = 8
N_EXPERTS = N_GROUPS * EXPERTS_PER_GROUP
TOP_K = 2
ROUTER_LANES = 128
NEG_BIG = -1e30
LOG2_E = 1.4426950408889634

VMEM_LIMIT = 56 * 1024 * 1024


def _rms(x, g):
    return x * lax.rsqrt(jnp.mean(x * x, axis=-1, keepdims=True) + NORM_EPS) * g


def _dot(a, b):
    return jnp.dot(a, b, preferred_element_type=F32)


def _dot_nt(a, b):
    return lax.dot_general(a, b, (((1,), (1,)), ((), ())), preferred_element_type=F32)


def _params(*sem):
    return pltpu.CompilerParams(dimension_semantics=sem, vmem_limit_bytes=VMEM_LIMIT)


def _const_spec(shape):
    return pl.BlockSpec(shape, lambda *_: (0,) * len(shape))


def _rope_kernel(pos_ref, freq_ref, cos_ref, sin_ref):
    ang = pos_ref[...].astype(F32) * freq_ref[...]
    cos_ref[...] = jnp.cos(ang)
    sin_ref[...] = jnp.sin(ang)


def _rope_tables(positions):
    half = QK_ROPE_DIM // 2
    t = positions.size
    inv_freq = 1.0 / (ROPE_THETA ** (jnp.arange(0, QK_ROPE_DIM, 2, dtype=F32) / QK_ROPE_DIM))
    rows = t * half // 128
    pos_rep = jnp.repeat(positions.reshape(-1), half).reshape(rows, 128)
    freq = jnp.tile(inv_freq, 128 // half).reshape(1, 128)
    cos, sin = pl.pallas_call(
        _rope_kernel,
        out_shape=(jax.ShapeDtypeStruct((rows, 128), F32),) * 2,
        name="rope_tables",
    )(pos_rep, freq)
    cos = cos.reshape(t, half)
    sin = sin.reshape(t, half)
    one = jnp.ones((t, QK_NOPE_DIM), F32)
    z16 = jnp.zeros((t, half), F32)
    z64 = jnp.zeros((t, QK_NOPE_DIM), F32)
    pad1 = jnp.ones((t, HEAD_PAD - QK_NOPE_DIM - QK_ROPE_DIM), F32)
    pad0 = jnp.zeros((t, HEAD_PAD - QK_NOPE_DIM - QK_ROPE_DIM), F32)
    tab_c = jnp.concatenate([one, cos, cos, pad1], axis=1)
    tab_a = jnp.concatenate([z64, sin, z16, pad0], axis=1)
    tab_b = jnp.concatenate([z64, z16, sin, pad0], axis=1)
    return tab_c, tab_a, tab_b


def _rope_apply(x, c, a, b):
    return x * c - pltpu.roll(x, HEAD_PAD - 16, 1) * a + pltpu.roll(x, 16, 1) * b


def _mem_kv_kernel(mem_ref, g_ref, w_ref, o_ref):
    mn = _rms(mem_ref[0], g_ref[0]).astype(BF16)
    o_ref[0, 0] = _dot(mn, w_ref[0]).astype(BF16)


def _mem_kv(mem, norm_mem, wkv):
    b, m, d = mem.shape
    l = norm_mem.shape[0]
    return pl.pallas_call(
        _mem_kv_kernel,
        grid=(l, b),
        in_specs=[
            pl.BlockSpec((1, m, d), lambda i, j: (j, 0, 0)),
            pl.BlockSpec((1, 1, d), lambda i, j: (i, 0, 0)),
            pl.BlockSpec((1, d, 2 * d), lambda i, j: (i, 0, 0)),
        ],
        out_specs=pl.BlockSpec((1, 1, m, 2 * d), lambda i, j: (i, j, 0, 0)),
        out_shape=jax.ShapeDtypeStruct((l, b, m, 2 * d), BF16),
        compiler_params=_params("arbitrary", "arbitrary"),
        name="mem_kv",
    )(mem, norm_mem.reshape(l, 1, d), wkv)


CONV_ROWS = 64


def _conv_mixer_kernel(h_ref, g_ref, win_ref, bin_ref, wdw_ref, bdw_ref, lng_ref, lnb_ref,
                       wout_ref, bout_ref, o_ref, cb_ref, cv_ref):
    ts, d = h_ref.shape[1], h_ref.shape[2]

    @pl.when(pl.program_id(1) == 0)
    def _():
        cb_ref[0, 0:CONV_HALO, :] = jnp.zeros((CONV_HALO, d), F32)

    h = h_ref[0]
    hn = _rms(h, g_ref[...]).astype(BF16)
    u = _dot(hn, win_ref[...]) + bin_ref[...]
    cb_ref[0, CONV_HALO:CONV_HALO + ts, :] = u[:, :d] * jax.nn.sigmoid(u[:, d:])
    ncopy = ts + CONV_HALO - SUBLANES
    for s in range(1, SUBLANES):
        cb_ref[s, 0:ncopy, :] = cb_ref[0, s:s + ncopy, :]

    base = CONV_HALO - (CONV_WIDTH - 1)

    def rows(i, carry):
        r0 = pl.multiple_of(i * CONV_ROWS, CONV_ROWS)
        for c0 in range(0, d, 128):
            acc = jnp.zeros((CONV_ROWS, 128), F32)
            for k in range(CONV_WIDTH):
                off = base + k
                row0 = pl.multiple_of(r0 + (off // SUBLANES) * SUBLANES, SUBLANES)
                tap = cb_ref[off % SUBLANES, pl.ds(row0, CONV_ROWS), c0:c0 + 128]
                acc = acc + tap * wdw_ref[k:k + 1, c0:c0 + 128]
            cv_ref[pl.ds(r0, CONV_ROWS), c0:c0 + 128] = acc
        return carry

    lax.fori_loop(0, ts // CONV_ROWS, rows, 0)
    cb_ref[0, 0:CONV_HALO, :] = cb_ref[0, ts:ts + CONV_HALO, :]

    v = cv_ref[...] + bdw_ref[...]
    mu = jnp.mean(v, axis=-1, keepdims=True)
    vc = v - mu
    var = jnp.mean(vc * vc, axis=-1, keepdims=True)
    y = vc * lax.rsqrt(var + NORM_EPS) * lng_ref[...] + lnb_ref[...]
    y = (y * jax.nn.sigmoid(y)).astype(BF16)
    o_ref[0] = h + _dot(y, wout_ref[...]) + bout_ref[...]


def _conv_mixer(h, g, w_in, b_in, w_dw, b_dw, ln_g, ln_b, w_out, b_out, ts=256):
    b, s, d = h.shape
    row = lambda v: v.reshape(1, -1)
    return pl.pallas_call(
        _conv_mixer_kernel,
        grid=(b, s // ts),
        in_specs=[
            pl.BlockSpec((1, ts, d), lambda i, j: (i, j, 0)),
            _const_spec((1, d)), _const_spec((d, 2 * d)), _const_spec((1, 2 * d)),
            _const_spec((CONV_WIDTH, d)), _const_spec((1, d)), _const_spec((1, d)), _const_spec((1, d)),
            _const_spec((d, d)), _const_spec((1, d)),
        ],
        out_specs=pl.BlockSpec((1, ts, d), lambda i, j: (i, j, 0)),
        out_shape=jax.ShapeDtypeStruct((b, s, d), F32),
        scratch_shapes=[pltpu.VMEM((SUBLANES, ts + CONV_HALO, d), F32), pltpu.VMEM((ts, d), F32)],
        compiler_params=_params("arbitrary", "arbitrary"),
        name="conv_mixer",
    )(h, row(g), w_in.astype(BF16), row(b_in), w_dw, row(b_dw), row(ln_g), row(ln_b),
      w_out.astype(BF16), row(b_out))


def _xattn_kernel(*refs, with_prefix):
    if with_prefix:
        h_ref, a_ref, wmo_ref, g_ref, wq_ref, kv_ref, wo_ref, o_ref = refs
        h = h_ref[0] + _dot(a_ref[0], wmo_ref[...])
    else:
        h_ref, g_ref, wq_ref, kv_ref, wo_ref, o_ref = refs
        h = h_ref[0]
    d = h.shape[-1]
    hd = d // XA_HEADS
    hn = _rms(h, g_ref[...]).astype(BF16)
    q = (_dot(hn, wq_ref[...]) * (1.0 / math.sqrt(hd))).astype(BF16)
    outs = []
    for hh in range(XA_HEADS):
        k = kv_ref[0, 0, :, hh * hd:(hh + 1) * hd]
        v = kv_ref[0, 0, :, d + hh * hd:d + (hh + 1) * hd]
        sc = _dot_nt(q[:, hh * hd:(hh + 1) * hd], k)
        p = jnp.exp(sc - jnp.max(sc, axis=-1, keepdims=True))
        inv = 1.0 / jnp.sum(p, axis=-1, keepdims=True)
        outs.append((_dot(p.astype(BF16), v) * inv).astype(BF16))
    o = jnp.concatenate(outs, axis=-1)
    o_ref[0] = h + _dot(o, wo_ref[...])


def _xattn(h, g, wq, kv, layer, wo, prefix=None, ts=512):
    b, s, d = h.shape
    m = kv.shape[2]
    tile = pl.BlockSpec((1, ts, d), lambda i, j: (i, j, 0))
    in_specs = [tile]
    args = [h]
    if prefix is not None:
        attn, w_mo = prefix
        in_specs += [tile, _const_spec((d, d))]
        args += [attn, w_mo.astype(BF16)]
    in_specs += [_const_spec((1, d)), _const_spec((d, d)),
                 pl.BlockSpec((1, 1, m, 2 * d), lambda i, j: (layer, i, 0, 0)), _const_spec((d, d))]
    args += [g.reshape(1, d), wq.astype(BF16), kv, wo.astype(BF16)]
    return pl.pallas_call(
        functools.partial(_xattn_kernel, with_prefix=prefix is not None),
        grid=(b, s // ts),
        in_specs=in_specs,
        out_specs=tile,
        out_shape=jax.ShapeDtypeStruct((b, s, d), F32),
        compiler_params=_params("arbitrary", "arbitrary"),
        name="xattn",
    )(*args)


ROUTE_TILE = 512
EXPERT_TILE = 256
COMBINE_TILE = 256
DISPATCH_TILE = 256
INFO_E1, INFO_E2, INFO_R1, INFO_R2, INFO_W1, INFO_W2 = range(6)
ROUTER_EXPERT_ROW = SUBLANES
ROUTER_ROWS = ROUTER_EXPERT_ROW + N_EXPERTS + SUBLANES


def _router_kernel(h_ref, g_ref, wrt_ref, brt_ref, hn_ref, info_ref, cnt_ref, carry_ref):
    tm = h_ref.shape[0]
    assert EXPERTS_PER_GROUP == SUBLANES

    @pl.when(pl.program_id(0) == 0)
    def _():
        carry_ref[...] = jnp.zeros_like(carry_ref)

    hn = _rms(h_ref[...], g_ref[...])
    hn_ref[...] = hn
    logits = lax.dot_general(wrt_ref[...], hn, (((1,), (1,)), ((), ())), preferred_element_type=F32,
                             precision=lax.Precision.HIGHEST) + brt_ref[:, 0:1]
    sub = lax.broadcasted_iota(I32, (SUBLANES, tm), 0)
    gl = jnp.where(sub < N_GROUPS, logits[0:SUBLANES], NEG_BIG)
    gmax = jnp.max(gl, axis=0, keepdims=True)
    gsum = jnp.sum(jnp.exp(gl - gmax), axis=0, keepdims=True)
    g_val = 1.0 / gsum
    g_idx = jnp.min(jnp.where(gl == gmax, sub, SUBLANES), axis=0, keepdims=True)
    el = logits[ROUTER_EXPERT_ROW:ROUTER_EXPERT_ROW + SUBLANES]
    for grp in range(1, N_GROUPS):
        lo = ROUTER_EXPERT_ROW + grp * SUBLANES
        el = jnp.where(g_idx == grp, logits[lo:lo + SUBLANES], el)
    m1 = jnp.max(el, axis=0, keepdims=True)
    esum = jnp.sum(jnp.exp(el - m1), axis=0, keepdims=True)
    i1 = jnp.min(jnp.where(el == m1, sub, SUBLANES), axis=0, keepdims=True)
    el2 = jnp.where(sub == i1, NEG_BIG, el)
    m2 = jnp.max(el2, axis=0, keepdims=True)
    i2 = jnp.min(jnp.where(el2 == m2, sub, SUBLANES), axis=0, keepdims=True)
    p1 = 1.0 / esum
    p2 = jnp.exp(m2 - m1) / esum
    w1 = g_val * (p1 / (p1 + p2))
    w2 = g_val * (p2 / (p1 + p2))
    e1 = g_idx * EXPERTS_PER_GROUP + i1
    e2 = g_idx * EXPERTS_PER_GROUP + i2
    erow = lax.broadcasted_iota(I32, (N_EXPERTS, tm), 0)
    hit = jnp.where((erow == e1) | (erow == e2), 1.0, 0.0)
    r = lax.broadcasted_iota(I32, (tm, tm), 0)
    c = lax.broadcasted_iota(I32, (tm, tm), 1)
    before = jnp.where(r < c, 1.0, 0.0).astype(BF16)
    tot = _dot(hit.astype(BF16), before) + carry_ref[:, 0:1]
    rank1 = jnp.sum(jnp.where(erow == e1, tot, 0.0), axis=0, keepdims=True)
    rank2 = jnp.sum(jnp.where(erow == e2, tot, 0.0), axis=0, keepdims=True)
    carry_ref[...] = carry_ref[...] + jnp.sum(hit, axis=1, keepdims=True)
    cnt_ref[...] = carry_ref[...]
    info = jnp.zeros((SUBLANES, tm), F32)
    for row, val in ((INFO_E1, e1.astype(F32)), (INFO_E2, e2.astype(F32)), (INFO_R1, rank1),
                     (INFO_R2, rank2), (INFO_W1, w1), (INFO_W2, w2)):
        info = jnp.where(sub == row, val, info)
    info_ref[...] = info


DISPATCH_SLOTS = 3


def _dispatch_kernel(p1_ref, p2_ref, hn_ref, xs_ref, buf, lsem, ssem):
    i = pl.program_id(0)
    n = pl.num_programs(0)
    tm = buf.shape[1]
    slot = i % DISPATCH_SLOTS

    def load(step):
        sl = step % DISPATCH_SLOTS
        return pltpu.make_async_copy(hn_ref.at[pl.ds(step * tm, tm)], buf.at[sl], lsem.at[sl])

    def wait_scatters(sl):
        cp = pltpu.make_async_copy(buf.at[sl], xs_ref.at[pl.ds(0, tm)], ssem.at[sl])
        for _ in range(TOP_K):
            cp.wait()

    @pl.when(i == 0)
    def _():
        load(0).start()

    @pl.when(i + 1 < n)
    def _():
        load(i + 1).start()

    load(i).wait()

    def tok(r, carry):
        tt = i * tm + r
        src = buf.at[slot, pl.ds(r, 1)]
        pltpu.make_async_copy(src, xs_ref.at[pl.ds(p1_ref[tt], 1)], ssem.at[slot]).start()
        pltpu.make_async_copy(src, xs_ref.at[pl.ds(p2_ref[tt], 1)], ssem.at[slot]).start()
        return carry

    lax.fori_loop(0, tm, tok, 0, unroll=8)

    @pl.when(i > 0)
    def _():
        wait_scatters((i - 1) % DISPATCH_SLOTS)

    @pl.when(i == n - 1)
    def _():
        wait_scatters(slot)


def _expert_kernel(ve_ref, vt_ref, lo_ref, hi_ref, nv_ref, x_ref, wg_ref, wu_ref, wd_ref, y_ref, wgu_s, wd_s):
    v = pl.program_id(0)
    f = wg_ref.shape[2]

    @pl.when(v < nv_ref[0])
    def _():
        pv = jnp.maximum(v - 1, 0)

        @pl.when((v == 0) | (ve_ref[v] != ve_ref[pv]))
        def _():
            wgu_s[:, :f] = wg_ref[0].astype(BF16)
            wgu_s[:, f:] = wu_ref[0].astype(BF16)
            wd_s[...] = wd_ref[0].astype(BF16)

        gu = _dot(x_ref[...].astype(BF16), wgu_s[...])
        gate = gu[:, :f]
        he = (gate * jax.nn.sigmoid(gate) * gu[:, f:]).astype(BF16)
        row = lax.broadcasted_iota(I32, (x_ref.shape[0], 1), 0)
        y = jnp.where((row >= lo_ref[v]) & (row < hi_ref[v]), _dot(he, wd_s[...]), 0.0)
        first = (v == 0) | (vt_ref[v] != vt_ref[pv])

        @pl.when(first)
        def _():
            y_ref[...] = y

        @pl.when(jnp.logical_not(first))
        def _():
            y_ref[...] += y


def _combine_kernel(p1_ref, p2_ref, h_ref, info_ref, fin_ref, ys_ref, o_ref, buf, sem, *, final):
    i = pl.program_id(0)
    n = pl.num_programs(0)
    tm = h_ref.shape[0]
    slot = i % 2

    def issue(step, sl):
        def tok(r, carry):
            tt = step * tm + r
            pltpu.make_async_copy(ys_ref.at[pl.ds(p1_ref[tt], 1)], buf.at[sl, 0, pl.ds(r, 1)], sem.at[sl]).start()
            pltpu.make_async_copy(ys_ref.at[pl.ds(p2_ref[tt], 1)], buf.at[sl, 1, pl.ds(r, 1)], sem.at[sl]).start()
            return carry

        lax.fori_loop(0, tm, tok, 0, unroll=8)

    @pl.when(i == 0)
    def _():
        issue(0, 0)

    @pl.when(i + 1 < n)
    def _():
        issue(i + 1, 1 - slot)

    for k in range(TOP_K):
        pltpu.make_async_copy(ys_ref.at[pl.ds(0, tm)], buf.at[slot, k], sem.at[slot]).wait()
    info = jnp.concatenate([info_ref[...], jnp.zeros((ROUTER_LANES - SUBLANES, tm), F32)], axis=0).T
    out = (h_ref[...] + info[:, INFO_W1:INFO_W1 + 1] * buf[slot, 0]
           + info[:, INFO_W2:INFO_W2 + 1] * buf[slot, 1])
    if final:
        out = _rms(out, fin_ref[...])
    o_ref[...] = out


def _moe(h, g, w_grp, b_grp, w_exp, b_exp, w_gate, w_up, w_down, final_norm, final):
    b, s, d = h.shape
    t = b * s
    f = w_gate.shape[-1]
    h2 = h.reshape(t, d)
    zrow = lambda n: jnp.zeros((n, d), F32)
    wrt = jnp.concatenate([w_grp.T, zrow(ROUTER_EXPERT_ROW - N_GROUPS), w_exp.T,
                           zrow(ROUTER_ROWS - ROUTER_EXPERT_ROW - N_EXPERTS)], axis=0)
    brt = jnp.concatenate([b_grp, jnp.zeros((ROUTER_EXPERT_ROW - N_GROUPS,), F32), b_exp,
                           jnp.zeros((ROUTER_ROWS - ROUTER_EXPERT_ROW - N_EXPERTS,), F32)])
    brt = jnp.broadcast_to(brt[:, None], (ROUTER_ROWS, ROUTER_LANES))

    hn, info, cnt = pl.pallas_call(
        _router_kernel,
        grid=(t // ROUTE_TILE,),
        in_specs=[pl.BlockSpec((ROUTE_TILE, d), lambda i: (i, 0)), _const_spec((1, d)),
                  _const_spec((ROUTER_ROWS, d)), _const_spec((ROUTER_ROWS, ROUTER_LANES))],
        out_specs=(pl.BlockSpec((ROUTE_TILE, d), lambda i: (i, 0)),
                   pl.BlockSpec((SUBLANES, ROUTE_TILE), lambda i: (0, i)),
                   _const_spec((N_EXPERTS, ROUTER_LANES))),
        out_shape=(jax.ShapeDtypeStruct((t, d), F32), jax.ShapeDtypeStruct((SUBLANES, t), F32),
                   jax.ShapeDtypeStruct((N_EXPERTS, ROUTER_LANES), F32)),
        scratch_shapes=[pltpu.VMEM((N_EXPERTS, ROUTER_LANES), F32)],
        compiler_params=_params("arbitrary"),
        name="moe_router",
    )(h2, g.reshape(1, d), wrt, brt)

    tm = EXPERT_TILE
    n_rows = TOP_K * t
    n_visits = n_rows // tm + N_EXPERTS - 1
    counts = cnt[:, 0].astype(I32)
    row_end = jnp.cumsum(counts)
    row_start = row_end - counts
    first_tile = row_start // tm
    e_visits = jnp.where(counts > 0, (row_end - 1) // tm - first_tile + 1, 0)
    v_end = jnp.cumsum(e_visits)
    v_start = v_end - e_visits
    n_v = v_end[-1:]
    vis = jnp.minimum(jnp.arange(n_visits, dtype=I32), n_v - 1)
    v_e = jnp.sum(vis[:, None] >= v_end[None, :], axis=1).astype(I32)
    v_tile = (first_tile[v_e] + vis - v_start[v_e]).astype(I32)
    v_lo = (jnp.maximum(row_start[v_e], v_tile * tm) - v_tile * tm).astype(I32)
    v_hi = (jnp.minimum(row_end[v_e], (v_tile + 1) * tm) - v_tile * tm).astype(I32)
    eid = jnp.arange(N_EXPERTS, dtype=I32)[:, None]
    start_of = lambda e: jnp.sum(jnp.where(e.astype(I32)[None, :] == eid, row_start[:, None], 0), axis=0)
    pos1 = start_of(info[INFO_E1]) + info[INFO_R1].astype(I32)
    pos2 = start_of(info[INFO_E2]) + info[INFO_R2].astype(I32)

    xs = pl.pallas_call(
        _dispatch_kernel,
        grid_spec=pltpu.PrefetchScalarGridSpec(
            num_scalar_prefetch=2, grid=(t // DISPATCH_TILE,),
            in_specs=[pl.BlockSpec(memory_space=pl.ANY)],
            out_specs=pl.BlockSpec(memory_space=pl.ANY),
            scratch_shapes=[pltpu.VMEM((DISPATCH_SLOTS, DISPATCH_TILE, d), F32),
                            pltpu.SemaphoreType.DMA((DISPATCH_SLOTS,)),
                            pltpu.SemaphoreType.DMA((DISPATCH_SLOTS,))],
        ),
        out_shape=jax.ShapeDtypeStruct((n_rows, d), F32),
        compiler_params=_params("arbitrary"),
        name="moe_dispatch",
    )(pos1, pos2, hn)

    by_tile = lambda v, ve, vt, lo, hi, nv: (vt[v], 0)
    by_expert = lambda v, ve, vt, lo, hi, nv: (ve[v], 0, 0)
    ys = pl.pallas_call(
        _expert_kernel,
        grid_spec=pltpu.PrefetchScalarGridSpec(
            num_scalar_prefetch=5, grid=(n_visits,),
            in_specs=[
                pl.BlockSpec((tm, d), by_tile),
                pl.BlockSpec((1, d, f), by_expert),
                pl.BlockSpec((1, d, f), by_expert),
                pl.BlockSpec((1, f, d), by_expert),
            ],
            out_specs=pl.BlockSpec((tm, d), by_tile),
            scratch_shapes=[pltpu.VMEM((d, 2 * f), BF16), pltpu.VMEM((f, d), BF16)],
        ),
        out_shape=jax.ShapeDtypeStruct((n_rows, d), F32),
        compiler_params=_params("arbitrary"),
        name="moe_experts",
    )(v_e, v_tile, v_lo, v_hi, n_v, xs, w_gate, w_up, w_down)

    tc = COMBINE_TILE
    out = pl.pallas_call(
        functools.partial(_combine_kernel, final=final),
        grid_spec=pltpu.PrefetchScalarGridSpec(
            num_scalar_prefetch=2, grid=(t // tc,),
            in_specs=[
                pl.BlockSpec((tc, d), lambda i, p1, p2: (i, 0)),
                pl.BlockSpec((SUBLANES, tc), lambda i, p1, p2: (0, i)),
                pl.BlockSpec((1, d), lambda i, p1, p2: (0, 0)),
                pl.BlockSpec(memory_space=pl.ANY),
            ],
            out_specs=pl.BlockSpec((tc, d), lambda i, p1, p2: (i, 0)),
            scratch_shapes=[pltpu.VMEM((2, TOP_K, tc, d), F32), pltpu.SemaphoreType.DMA((2,))],
        ),
        out_shape=jax.ShapeDtypeStruct((t, d), F32),
        compiler_params=_params("arbitrary"),
        name="moe_combine",
    )(pos1, pos2, h2, info, final_norm.reshape(1, d), ys)
    return out.reshape(b, s, d)


MLA_TILE = 256
VT_ROWS = V_HEAD_DIM + 16


def _mla_proj_kernel(h_ref, g_ref, wdn_ref, qn_ref, wuq_ref, kvn_ref, wuk_ref, wuvt_ref,
                     tc_ref, ta_ref, tb_ref, q_ref, k_ref, vt_ref):
    hn = _rms(h_ref[0], g_ref[...]).astype(BF16)
    dn = _dot(hn, wdn_ref[...])
    c, a, b = tc_ref[...], ta_ref[...], tb_ref[...]
    scale = LOG2_E / math.sqrt(QK_NOPE_DIM + QK_ROPE_DIM)
    kv_lo = Q_LORA_RANK
    kr_lo = Q_LORA_RANK + KV_LORA_RANK
    cq = _rms(dn[:, :kv_lo], qn_ref[...]).astype(BF16)
    q = _dot(cq, wuq_ref[...])
    for hh in range(MLA_HEADS):
        sl = slice(hh * HEAD_PAD, (hh + 1) * HEAD_PAD)
        q_ref[0, :, sl] = (_rope_apply(q[:, sl], c, a, b) * scale).astype(BF16)
    ckv = _rms(dn[:, kv_lo:kr_lo], kvn_ref[...]).astype(BF16)
    kn = _dot(ckv, wuk_ref[...])
    kr = _rope_apply(dn[:, kr_lo:kr_lo + HEAD_PAD], c, a, b)
    for hh in range(MLA_HEADS):
        sl = slice(hh * HEAD_PAD, (hh + 1) * HEAD_PAD)
        k_ref[0, :, sl] = (kn[:, sl] + kr).astype(BF16)
    vt = _dot_nt(wuvt_ref[...], ckv).astype(BF16)
    ones = jnp.ones((VT_ROWS - V_HEAD_DIM, vt.shape[1]), BF16)
    for hh in range(MLA_HEADS):
        vt_ref[0, hh, 0, 0:V_HEAD_DIM, :] = vt[hh * V_HEAD_DIM:(hh + 1) * V_HEAD_DIM, :]
        vt_ref[0, hh, 0, V_HEAD_DIM:VT_ROWS, :] = ones


def _mla_proj(h, g, w_down, q_norm, w_uq, kv_norm, w_ukv, tabs):
    b, s, d = h.shape
    ts = MLA_TILE
    nh, dn_, dr, dv = MLA_HEADS, QK_NOPE_DIM, QK_ROPE_DIM, V_HEAD_DIM
    kr_lo = Q_LORA_RANK + KV_LORA_RANK
    w_dn = jnp.concatenate([
        w_down[:, :kr_lo], jnp.zeros((d, dn_), F32), w_down[:, kr_lo:],
        jnp.zeros((d, HEAD_PAD - dn_ - dr), F32)], axis=1).astype(BF16)
    wq = w_uq.reshape(Q_LORA_RANK, nh, dn_ + dr)
    wq = jnp.pad(wq, ((0, 0), (0, 0), (0, HEAD_PAD - dn_ - dr))).reshape(Q_LORA_RANK, nh * HEAD_PAD).astype(BF16)
    wkv = w_ukv.reshape(KV_LORA_RANK, nh, dn_ + dv)
    wk = jnp.pad(wkv[:, :, :dn_], ((0, 0), (0, 0), (0, HEAD_PAD - dn_))).reshape(KV_LORA_RANK, nh * HEAD_PAD).astype(BF16)
    wvt = wkv[:, :, dn_:].reshape(KV_LORA_RANK, nh * dv).T.astype(BF16)
    tab_c, tab_a, tab_b = tabs
    nblk = s // ts
    tab = pl.BlockSpec((ts, HEAD_PAD), lambda i, j: (i * nblk + j, 0))
    tile = lambda w: pl.BlockSpec((1, ts, w), lambda i, j: (i, j, 0))
    return pl.pallas_call(
        _mla_proj_kernel,
        grid=(b, nblk),
        in_specs=[tile(d), _const_spec((1, d)), _const_spec(w_dn.shape), _const_spec((1, Q_LORA_RANK)),
                  _const_spec(wq.shape), _const_spec((1, KV_LORA_RANK)), _const_spec(wk.shape),
                  _const_spec(wvt.shape), tab, tab, tab],
        out_specs=(tile(nh * HEAD_PAD), tile(nh * HEAD_PAD),
                   pl.BlockSpec((1, nh, 1, VT_ROWS, ts), lambda i, j: (i, 0, j, 0, 0))),
        out_shape=(jax.ShapeDtypeStruct((b, s, nh * HEAD_PAD), BF16),
                   jax.ShapeDtypeStruct((b, s, nh * HEAD_PAD), BF16),
                   jax.ShapeDtypeStruct((b, nh, nblk, VT_ROWS, ts), BF16)),
        compiler_params=_params("arbitrary", "arbitrary"),
        name="mla_proj",
    )(h, g.reshape(1, d), w_dn, q_norm.reshape(1, -1), wq, kv_norm.reshape(1, -1), wk, wvt,
      tab_c, tab_a, tab_b)


def _flash_kernel(q_ref, k_ref, vt_ref, o_ref, sa_ref, sb_ref, acc_ref):
    tq = q_ref.shape[1]
    tk = vt_ref.shape[4]
    qi = pl.program_id(2)

    def scores(j, s_ref, c0=0):
        k2 = k_ref[0, pl.ds(pl.multiple_of(j * tk, tk), tk), :]
        for hh in range(2):
            sl = slice(hh * HEAD_PAD, (hh + 1) * HEAD_PAD)
            s_ref[hh, :, c0:] = _dot_nt(k2[:, sl], q_ref[0, c0:, sl])

    def consume(j, s_ref, ms, c0=0, diag=False):
        out = []
        for hh in range(2):
            st = s_ref[hh, :, c0:]
            if diag:
                key = lax.broadcasted_iota(I32, st.shape, 0)
                qry = lax.broadcasted_iota(I32, st.shape, 1)
                st = jnp.where(key <= qry, st, NEG_BIG)
            m = ms[hh][:, c0:]
            m_new = jnp.maximum(m, jnp.max(st, axis=0, keepdims=True))
            alpha = jnp.exp2(m - m_new)
            p = jnp.exp2(st - m_new).astype(BF16)
            acc_ref[hh, :, c0:] = alpha * acc_ref[hh, :, c0:] + _dot(vt_ref[0, hh, j], p)
            out.append(m_new if c0 == 0 else jnp.concatenate([ms[hh][:, :c0], m_new], axis=1))
        return tuple(out)

    acc_ref[...] = jnp.zeros_like(acc_ref)
    ms = tuple(jnp.full((1, tq), NEG_BIG, F32) for _ in range(2))
    scores(0, sa_ref)

    def two_tiles(i, ms):
        scores(2 * i + 1, sb_ref)
        ms = consume(2 * i, sa_ref, ms)
        scores(2 * i + 2, sa_ref)
        return consume(2 * i + 1, sb_ref, ms)

    assert tq == 2 * tk
    ms = lax.fori_loop(0, qi, two_tiles, ms)
    scores(2 * qi + 1, sb_ref, c0=tk)
    ms = consume(2 * qi, sa_ref, ms, diag=True)
    consume(2 * qi + 1, sb_ref, ms, c0=tk, diag=True)
    heads = [acc_ref[hh, 0:V_HEAD_DIM, :] * (1.0 / acc_ref[hh, V_HEAD_DIM:V_HEAD_DIM + 1, :]) for hh in range(2)]
    o_ref[0] = jnp.concatenate(heads, axis=0).T.astype(BF16)


def _flash(q, k, vt, tq=512):
    b, s, _ = q.shape
    _, nh, nblk, rows, tk = vt.shape
    return pl.pallas_call(
        _flash_kernel,
        grid=(b, nh // 2, s // tq),
        in_specs=[
            pl.BlockSpec((1, tq, 2 * HEAD_PAD), lambda i, p, j: (i, j, p)),
            pl.BlockSpec((1, s, 2 * HEAD_PAD), lambda i, p, j: (i, 0, p)),
            pl.BlockSpec((1, 2, nblk, rows, tk), lambda i, p, j: (i, p, 0, 0, 0)),
        ],
        out_specs=pl.BlockSpec((1, tq, 2 * V_HEAD_DIM), lambda i, p, j: (i, j, p)),
        out_shape=jax.ShapeDtypeStruct((b, s, nh * V_HEAD_DIM), BF16),
        scratch_shapes=[pltpu.VMEM((2, tk, tq), F32), pltpu.VMEM((2, tk, tq), F32),
                        pltpu.VMEM((2, rows, tq), F32)],
        compiler_params=_params("arbitrary", "arbitrary", "arbitrary"),
        name="flash",
    )(q, k, vt)


def kernel(x, mem, positions, norm_mix, conv_w_in, conv_b_in, conv_w_dw, conv_b_dw, conv_ln_g, conv_ln_b, conv_w_out, conv_b_out, mla_w_down, mla_q_norm, mla_w_uq, mla_kv_norm, mla_w_ukv, mla_w_o, norm_xa, norm_mem, xa_wq, xa_wkv, xa_wo, norm_ffn, moe_w_grp, moe_b_grp, moe_w_exp, moe_b_exp, moe_w_gate, moe_w_up, moe_w_down, final_norm):
    tabs = _rope_tables(positions)
    kv_mem = _mem_kv(mem, norm_mem, xa_wkv.astype(BF16))

    def ffn(h, i, final):
        return _moe(h, norm_ffn[i], moe_w_grp[i], moe_b_grp[i], moe_w_exp[i], moe_b_exp[i],
                    moe_w_gate[i], moe_w_up[i], moe_w_down[i], final_norm, final)

    h = _conv_mixer(x, norm_mix[0], conv_w_in[0], conv_b_in[0], conv_w_dw[0], conv_b_dw[0],
                    conv_ln_g[0], conv_ln_b[0], conv_w_out[0], conv_b_out[0])
    h = _xattn(h, norm_xa[0], xa_wq[0], kv_mem, 0, xa_wo[0])
    h = ffn(h, 0, False)
    q, k, vt = _mla_proj(h, norm_mix[1], mla_w_down[0], mla_q_norm[0], mla_w_uq[0], mla_kv_norm[0],
                         mla_w_ukv[0], tabs)
    attn = _flash(q, k, vt)
    h = _xattn(h, norm_xa[1], xa_wq[1], kv_mem, 1, xa_wo[1], prefix=(attn, mla_w_o[0]))
    return ffn(h, 1, True)
```

```python
import functools
import math

import jax
import jax.numpy as jnp
from jax import lax
from jax.experimental import pallas as pl
from jax.experimental.pallas import tpu as pltpu

F32 = jnp.float32
BF16 = jnp.bfloat16
I32 = jnp.int32

NORM_EPS = 1e-6
CONV_WIDTH = 31
CONV_HALO = 32
SUBLANES = 8
MLA_HEADS = 16
QK_NOPE_DIM = 64
QK_ROPE_DIM = 32
V_HEAD_DIM = 64
Q_LORA_RANK = 384
KV_LORA_RANK = 256
ROPE_THETA = 10000.0
HEAD_PAD = 128
XA_HEADS = 4
N_GROUPS = 4
EXPERTS_PER_GROUP = 8
N_EXPERTS = N_GROUPS * EXPERTS_PER_GROUP
TOP_K = 2
ROUTER_LANES = 128
NEG_BIG = -1e30
LOG2_E = 1.4426950408889634

VMEM_LIMIT = 56 * 1024 * 1024


def _rms(x, g):
    return x * lax.rsqrt(jnp.mean(x * x, axis=-1, keepdims=True) + NORM_EPS) * g


def _dot(a, b):
    return jnp.dot(a, b, preferred_element_type=F32)


def _dot_nt(a, b):
    return lax.dot_general(a, b, (((1,), (1,)), ((), ())), preferred_element_type=F32)


def _params(*sem):
    return pltpu.CompilerParams(dimension_semantics=sem, vmem_limit_bytes=VMEM_LIMIT)


def _const_spec(shape):
    return pl.BlockSpec(shape, lambda *_: (0,) * len(shape))


def _rope_kernel(pos_ref, freq_ref, cos_ref, sin_ref):
    ang = pos_ref[...].astype(F32) * freq_ref[...]
    cos_ref[...] = jnp.cos(ang)
    sin_ref[...] = jnp.sin(ang)


def _rope_tables(positions):
    half = QK_ROPE_DIM // 2
    t = positions.size
    inv_freq = 1.0 / (ROPE_THETA ** (jnp.arange(0, QK_ROPE_DIM, 2, dtype=F32) / QK_ROPE_DIM))
    rows = t * half // 128
    pos_rep = jnp.repeat(positions.reshape(-1), half).reshape(rows, 128)
    freq = jnp.tile(inv_freq, 128 // half).reshape(1, 128)
    cos, sin = pl.pallas_call(
        _rope_kernel,
        out_shape=(jax.ShapeDtypeStruct((rows, 128), F32),) * 2,
        name="rope_tables",
    )(pos_rep, freq)
    cos = cos.reshape(t, half)
    sin = sin.reshape(t, half)
    one = jnp.ones((t, QK_NOPE_DIM), F32)
    z16 = jnp.zeros((t, half), F32)
    z64 = jnp.zeros((t, QK_NOPE_DIM), F32)
    pad1 = jnp.ones((t, HEAD_PAD - QK_NOPE_DIM - QK_ROPE_DIM), F32)
    pad0 = jnp.zeros((t, HEAD_PAD - QK_NOPE_DIM - QK_ROPE_DIM), F32)
    tab_c = jnp.concatenate([one, cos, cos, pad1], axis=1)
    tab_a = jnp.concatenate([z64, sin, z16, pad0], axis=1)
    tab_b = jnp.concatenate([z64, z16, sin, pad0], axis=1)
    return tab_c, tab_a, tab_b


def _rope_apply(x, c, a, b):
    return x * c - pltpu.roll(x, HEAD_PAD - 16, 1) * a + pltpu.roll(x, 16, 1) * b


def _mem_kv_kernel(mem_ref, g_ref, w_ref, o_ref):
    mn = _rms(mem_ref[0], g_ref[0]).astype(BF16)
    o_ref[0, 0] = _dot(mn, w_ref[0]).astype(BF16)


def _mem_kv(mem, norm_mem, wkv):
    b, m, d = mem.shape
    l = norm_mem.shape[0]
    return pl.pallas_call(
        _mem_kv_kernel,
        grid=(l, b),
        in_specs=[
            pl.BlockSpec((1, m, d), lambda i, j: (j, 0, 0)),
            pl.BlockSpec((1, 1, d), lambda i, j: (i, 0, 0)),
            pl.BlockSpec((1, d, 2 * d), lambda i, j: (i, 0, 0)),
        ],
        out_specs=pl.BlockSpec((1, 1, m, 2 * d), lambda i, j: (i, j, 0, 0)),
        out_shape=jax.ShapeDtypeStruct((l, b, m, 2 * d), BF16),
        compiler_params=_params("arbitrary", "arbitrary"),
        name="mem_kv",
    )(mem, norm_mem.reshape(l, 1, d), wkv)


CONV_ROWS = 64


def _conv_mixer_kernel(h_ref, g_ref, win_ref, bin_ref, wdw_ref, bdw_ref, lng_ref, lnb_ref,
                       wout_ref, bout_ref, o_ref, cb_ref, cv_ref):
    ts, d = h_ref.shape[1], h_ref.shape[2]

    @pl.when(pl.program_id(1) == 0)
    def _():
        cb_ref[0, 0:CONV_HALO, :] = jnp.zeros((CONV_HALO, d), F32)

    h = h_ref[0]
    hn = _rms(h, g_ref[...]).astype(BF16)
    u = _dot(hn, win_ref[...]) + bin_ref[...]
    cb_ref[0, CONV_HALO:CONV_HALO + ts, :] = u[:, :d] * jax.nn.sigmoid(u[:, d:])
    ncopy = ts + CONV_HALO - SUBLANES
    for s in range(1, SUBLANES):
        cb_ref[s, 0:ncopy, :] = cb_ref[0, s:s + ncopy, :]

    base = CONV_HALO - (CONV_WIDTH - 1)

    def rows(i, carry):
        r0 = pl.multiple_of(i * CONV_ROWS, CONV_ROWS)
        for c0 in range(0, d, 128):
            acc = jnp.zeros((CONV_ROWS, 128), F32)
            for k in range(CONV_WIDTH):
                off = base + k
                row0 = pl.multiple_of(r0 + (off // SUBLANES) * SUBLANES, SUBLANES)
                tap = cb_ref[off % SUBLANES, pl.ds(row0, CONV_ROWS), c0:c0 + 128]
                acc = acc + tap * wdw_ref[k:k + 1, c0:c0 + 128]
            cv_ref[pl.ds(r0, CONV_ROWS), c0:c0 + 128] = acc
        return carry

    lax.fori_loop(0, ts // CONV_ROWS, rows, 0)
    cb_ref[0, 0:CONV_HALO, :] = cb_ref[0, ts:ts + CONV_HALO, :]

    v = cv_ref[...] + bdw_ref[...]
    mu = jnp.mean(v, axis=-1, keepdims=True)
    vc = v - mu
    var = jnp.mean(vc * vc, axis=-1, keepdims=True)
    y = vc * lax.rsqrt(var + NORM_EPS) * lng_ref[...] + lnb_ref[...]
    y = (y * jax.nn.sigmoid(y)).astype(BF16)
    o_ref[0] = h + _dot(y, wout_ref[...]) + bout_ref[...]


def _conv_mixer(h, g, w_in, b_in, w_dw, b_dw, ln_g, ln_b, w_out, b_out, ts=256):
    b, s, d = h.shape
    row = lambda v: v.reshape(1, -1)
    return pl.pallas_call(
        _conv_mixer_kernel,
        grid=(b, s // ts),
        in_specs=[
            pl.BlockSpec((1, ts, d), lambda i, j: (i, j, 0)),
            _const_spec((1, d)), _const_spec((d, 2 * d)), _const_spec((1, 2 * d)),
            _const_spec((CONV_WIDTH, d)), _const_spec((1, d)), _const_spec((1, d)), _const_spec((1, d)),
            _const_spec((d, d)), _const_spec((1, d)),
        ],
        out_specs=pl.BlockSpec((1, ts, d), lambda i, j: (i, j, 0)),
        out_shape=jax.ShapeDtypeStruct((b, s, d), F32),
        scratch_shapes=[pltpu.VMEM((SUBLANES, ts + CONV_HALO, d), F32), pltpu.VMEM((ts, d), F32)],
        compiler_params=_params("arbitrary", "arbitrary"),
        name="conv_mixer",
    )(h, row(g), w_in.astype(BF16), row(b_in), w_dw, row(b_dw), row(ln_g), row(ln_b),
      w_out.astype(BF16), row(b_out))


def _xattn_kernel(*refs, with_prefix):
    if with_prefix:
        h_ref, a_ref, wmo_ref, g_ref, wq_ref, kv_ref, wo_ref, o_ref = refs
        h = h_ref[0] + _dot(a_ref[0], wmo_ref[...])
    else:
        h_ref, g_ref, wq_ref, kv_ref, wo_ref, o_ref = refs
        h = h_ref[0]
    d = h.shape[-1]
    hd = d // XA_HEADS
    hn = _rms(h, g_ref[...]).astype(BF16)
    q = (_dot(hn, wq_ref[...]) * (1.0 / math.sqrt(hd))).astype(BF16)
    outs = []
    for hh in range(XA_HEADS):
        k = kv_ref[0, 0, :, hh * hd:(hh + 1) * hd]
        v = kv_ref[0, 0, :, d + hh * hd:d + (hh + 1) * hd]
        sc = _dot_nt(q[:, hh * hd:(hh + 1) * hd], k)
        p = jnp.exp(sc - jnp.max(sc, axis=-1, keepdims=True))
        inv = 1.0 / jnp.sum(p, axis=-1, keepdims=True)
        outs.append((_dot(p.astype(BF16), v) * inv).astype(BF16))
    o = jnp.concatenate(outs, axis=-1)
    o_ref[0] = h + _dot(o, wo_ref[...])


def _xattn(h, g, wq, kv, layer, wo, prefix=None, ts=512):
    b, s, d = h.shape
    m = kv.shape[2]
    tile = pl.BlockSpec((1, ts, d), lambda i, j: (i, j, 0))
    in_specs = [tile]
    args = [h]
    if prefix is not None:
        attn, w_mo = prefix
        in_specs += [tile, _const_spec((d, d))]
        args += [attn, w_mo.astype(BF16)]
    in_specs += [_const_spec((1, d)), _const_spec((d, d)),
                 pl.BlockSpec((1, 1, m, 2 * d), lambda i, j: (layer, i, 0, 0)), _const_spec((d, d))]
    args += [g.reshape(1, d), wq.astype(BF16), kv, wo.astype(BF16)]
    return pl.pallas_call(
        functools.partial(_xattn_kernel, with_prefix=prefix is not None),
        grid=(b, s // ts),
        in_specs=in_specs,
        out_specs=tile,
        out_shape=jax.ShapeDtypeStruct((b, s, d), F32),
        compiler_params=_params("arbitrary", "arbitrary"),
        name="xattn",
    )(*args)


ROUTE_TILE = 512
EXPERT_TILE = 256
COMBINE_TILE = 256
DISPATCH_TILE = 256
LANES = 128


def _load_slabs(ref, rows, d):
    n = d // LANES
    return jnp.concatenate([ref[pl.ds(j, rows, stride=n), :] for j in range(n)], axis=1)


def _store_slabs(ref, val, accumulate=False):
    rows, d = val.shape
    n = d // LANES
    for j in range(n):
        chunk = val[:, j * LANES:(j + 1) * LANES]
        if accumulate:
            chunk = chunk + ref[pl.ds(j, rows, stride=n), :]
        ref[pl.ds(j, rows, stride=n), :] = chunk


INFO_E1, INFO_E2, INFO_R1, INFO_R2, INFO_W1, INFO_W2 = range(6)
ROUTER_EXPERT_ROW = SUBLANES
ROUTER_ROWS = ROUTER_EXPERT_ROW + N_EXPERTS + SUBLANES


def _router_kernel(h_ref, g_ref, wrt_ref, brt_ref, hn_ref, info_ref, cnt_ref, carry_ref):
    tm = h_ref.shape[0]
    assert EXPERTS_PER_GROUP == SUBLANES

    @pl.when(pl.program_id(0) == 0)
    def _():
        carry_ref[...] = jnp.zeros_like(carry_ref)

    hn = _rms(h_ref[...], g_ref[...])
    _store_slabs(hn_ref, hn)
    logits = lax.dot_general(wrt_ref[...], hn, (((1,), (1,)), ((), ())), preferred_element_type=F32,
                             precision=lax.Precision.HIGHEST) + brt_ref[:, 0:1]
    sub = lax.broadcasted_iota(I32, (SUBLANES, tm), 0)
    gl = jnp.where(sub < N_GROUPS, logits[0:SUBLANES], NEG_BIG)
    gmax = jnp.max(gl, axis=0, keepdims=True)
    gsum = jnp.sum(jnp.exp(gl - gmax), axis=0, keepdims=True)
    g_val = 1.0 / gsum
    g_idx = jnp.min(jnp.where(gl == gmax, sub, SUBLANES), axis=0, keepdims=True)
    el = logits[ROUTER_EXPERT_ROW:ROUTER_EXPERT_ROW + SUBLANES]
    for grp in range(1, N_GROUPS):
        lo = ROUTER_EXPERT_ROW + grp * SUBLANES
        el = jnp.where(g_idx == grp, logits[lo:lo + SUBLANES], el)
    m1 = jnp.max(el, axis=0, keepdims=True)
    esum = jnp.sum(jnp.exp(el - m1), axis=0, keepdims=True)
    i1 = jnp.min(jnp.where(el == m1, sub, SUBLANES), axis=0, keepdims=True)
    el2 = jnp.where(sub == i1, NEG_BIG, el)
    m2 = jnp.max(el2, axis=0, keepdims=True)
    i2 = jnp.min(jnp.where(el2 == m2, sub, SUBLANES), axis=0, keepdims=True)
    p1 = 1.0 / esum
    p2 = jnp.exp(m2 - m1) / esum
    w1 = g_val * (p1 / (p1 + p2))
    w2 = g_val * (p2 / (p1 + p2))
    e1 = g_idx * EXPERTS_PER_GROUP + i1
    e2 = g_idx * EXPERTS_PER_GROUP + i2
    erow = lax.broadcasted_iota(I32, (N_EXPERTS, tm), 0)
    hit = jnp.where((erow == e1) | (erow == e2), 1.0, 0.0)
    r = lax.broadcasted_iota(I32, (tm, tm), 0)
    c = lax.broadcasted_iota(I32, (tm, tm), 1)
    before = jnp.where(r < c, 1.0, 0.0).astype(BF16)
    tot = _dot(hit.astype(BF16), before) + carry_ref[:, 0:1]
    rank1 = jnp.sum(jnp.where(erow == e1, tot, 0.0), axis=0, keepdims=True)
    rank2 = jnp.sum(jnp.where(erow == e2, tot, 0.0), axis=0, keepdims=True)
    carry_ref[...] = carry_ref[...] + jnp.sum(hit, axis=1, keepdims=True)
    cnt_ref[...] = carry_ref[...]
    info = jnp.zeros((SUBLANES, tm), F32)
    for row, val in ((INFO_E1, e1.astype(F32)), (INFO_E2, e2.astype(F32)), (INFO_R1, rank1),
                     (INFO_R2, rank2), (INFO_W1, w1), (INFO_W2, w2)):
        info = jnp.where(sub == row, val, info)
    info_ref[...] = info


DISPATCH_SLOTS = 3


def _dispatch_kernel(p1_ref, p2_ref, hn_ref, xs_ref, buf, lsem, ssem):
    i = pl.program_id(0)
    n = pl.num_programs(0)
    slab = xs_ref.shape[1]
    rows = buf.shape[1]
    tm = rows // slab
    slot = i % DISPATCH_SLOTS

    def load(step):
        sl = step % DISPATCH_SLOTS
        return pltpu.make_async_copy(hn_ref.at[pl.ds(step * rows, rows)], buf.at[sl], lsem.at[sl])

    def wait_scatters(sl):
        cp = pltpu.make_async_copy(buf.at[sl], hn_ref.at[pl.ds(0, rows)], ssem.at[sl])
        for _ in range(TOP_K):
            cp.wait()

    @pl.when(i == 0)
    def _():
        load(0).start()

    @pl.when(i + 1 < n)
    def _():
        load(i + 1).start()

    load(i).wait()

    def tok(r, carry):
        tt = i * tm + r
        src = buf.at[slot, pl.ds(pl.multiple_of(r * slab, slab), slab)]
        pltpu.make_async_copy(src, xs_ref.at[p1_ref[tt]], ssem.at[slot]).start()
        pltpu.make_async_copy(src, xs_ref.at[p2_ref[tt]], ssem.at[slot]).start()
        return carry

    lax.fori_loop(0, tm, tok, 0, unroll=8)

    @pl.when(i > 0)
    def _():
        wait_scatters((i - 1) % DISPATCH_SLOTS)

    @pl.when(i == n - 1)
    def _():
        wait_scatters(slot)


def _expert_kernel(ve_ref, vt_ref, lo_ref, hi_ref, nv_ref, x_ref, wg_ref, wu_ref, wd_ref, y_ref,
                   wgu_s, wd_s, gu_s):
    v = pl.program_id(0)
    d, f = wg_ref.shape[2], wg_ref.shape[3]
    tm = x_ref.shape[0] * LANES // d

    @pl.when(v < nv_ref[0])
    def _():
        pv = jnp.maximum(v - 1, 0)

        @pl.when((v == 0) | (ve_ref[v] != ve_ref[pv]))
        def _():
            wgu_s[:, :f] = wg_ref[0, 0].astype(BF16)
            wgu_s[:, f:] = wu_ref[0, 0].astype(BF16)
            wd_s[...] = wd_ref[0, 0].astype(BF16)

        half = tm // 2
        slab = d // LANES

        @pl.when((v == 0) | (vt_ref[v] != vt_ref[pv]))
        def _():
            y_ref[...] = jnp.zeros_like(y_ref)

        for hh in range(2):
            x = _load_slabs(x_ref.at[pl.ds(hh * half * slab, half * slab)], half, d)
            gu_s[hh] = _dot(x.astype(BF16), wgu_s[...])
        for hh in range(2):
            gate = gu_s[hh, :, :f]
            he = (gate * jax.nn.sigmoid(gate) * gu_s[hh, :, f:]).astype(BF16)
            row = lax.broadcasted_iota(I32, (half, 1), 0) + hh * half
            y = jnp.where((row >= lo_ref[v]) & (row < hi_ref[v]), _dot(he, wd_s[...]), 0.0)
            _store_slabs(y_ref.at[pl.ds(hh * half * slab, half * slab)], y, accumulate=True)


def _combine_kernel(p1_ref, p2_ref, h_ref, info_ref, fin_ref, ys_ref, ys2d_ref, o_ref, buf, sem, *, final):
    i = pl.program_id(0)
    n = pl.num_programs(0)
    tm, d = h_ref.shape
    slab = ys_ref.shape[1]
    slot = i % 2

    def issue(step, sl):
        def tok(r, carry):
            tt = step * tm + r
            dst = pl.ds(pl.multiple_of(r * slab, slab), slab)
            pltpu.make_async_copy(ys_ref.at[p1_ref[tt]], buf.at[sl, 0, dst], sem.at[sl]).start()
            pltpu.make_async_copy(ys_ref.at[p2_ref[tt]], buf.at[sl, 1, dst], sem.at[sl]).start()
            return carry

        lax.fori_loop(0, tm, tok, 0, unroll=8)

    @pl.when(i == 0)
    def _():
        issue(0, 0)

    @pl.when(i + 1 < n)
    def _():
        issue(i + 1, 1 - slot)

    for k in range(TOP_K):
        pltpu.make_async_copy(ys2d_ref.at[pl.ds(0, tm * slab)], buf.at[slot, k], sem.at[slot]).wait()
    info = jnp.concatenate([info_ref[...], jnp.zeros((ROUTER_LANES - SUBLANES, tm), F32)], axis=0).T
    out = (h_ref[...] + info[:, INFO_W1:INFO_W1 + 1] * _load_slabs(buf.at[slot, 0], tm, d)
           + info[:, INFO_W2:INFO_W2 + 1] * _load_slabs(buf.at[slot, 1], tm, d))
    if final:
        out = _rms(out, fin_ref[...])
    o_ref[...] = out


def _moe(h, g, w_grp, b_grp, w_exp, b_exp, w_gate, w_up, w_down, layer, final_norm, final):
    b, s, d = h.shape
    t = b * s
    f = w_gate.shape[-1]
    slab = d // LANES
    assert slab == SUBLANES
    h2 = h.reshape(t, d)
    zrow = lambda n: jnp.zeros((n, d), F32)
    wrt = jnp.concatenate([w_grp.T, zrow(ROUTER_EXPERT_ROW - N_GROUPS), w_exp.T,
                           zrow(ROUTER_ROWS - ROUTER_EXPERT_ROW - N_EXPERTS)], axis=0)
    brt = jnp.concatenate([b_grp, jnp.zeros((ROUTER_EXPERT_ROW - N_GROUPS,), F32), b_exp,
                           jnp.zeros((ROUTER_ROWS - ROUTER_EXPERT_ROW - N_EXPERTS,), F32)])
    brt = jnp.broadcast_to(brt[:, None], (ROUTER_ROWS, ROUTER_LANES))

    hn, info, cnt = pl.pallas_call(
        _router_kernel,
        grid=(t // ROUTE_TILE,),
        in_specs=[pl.BlockSpec((ROUTE_TILE, d), lambda i: (i, 0)), _const_spec((1, d)),
                  _const_spec((ROUTER_ROWS, d)), _const_spec((ROUTER_ROWS, ROUTER_LANES))],
        out_specs=(pl.BlockSpec((ROUTE_TILE * slab, LANES), lambda i: (i, 0)),
                   pl.BlockSpec((SUBLANES, ROUTE_TILE), lambda i: (0, i)),
                   _const_spec((N_EXPERTS, ROUTER_LANES))),
        out_shape=(jax.ShapeDtypeStruct((t * slab, LANES), F32), jax.ShapeDtypeStruct((SUBLANES, t), F32),
                   jax.ShapeDtypeStruct((N_EXPERTS, ROUTER_LANES), F32)),
        scratch_shapes=[pltpu.VMEM((N_EXPERTS, ROUTER_LANES), F32)],
        compiler_params=_params("arbitrary"),
        name="moe_router",
    )(h2, g.reshape(1, d), wrt, brt)

    tm = EXPERT_TILE
    n_rows = TOP_K * t
    n_visits = n_rows // tm + N_EXPERTS - 1
    counts = cnt[:, 0].astype(I32)
    row_end = jnp.cumsum(counts)
    row_start = row_end - counts
    first_tile = row_start // tm
    e_visits = jnp.where(counts > 0, (row_end - 1) // tm - first_tile + 1, 0)
    v_end = jnp.cumsum(e_visits)
    v_start = v_end - e_visits
    n_v = v_end[-1:]
    vis = jnp.minimum(jnp.arange(n_visits, dtype=I32), n_v - 1)
    v_e = jnp.sum(vis[:, None] >= v_end[None, :], axis=1).astype(I32)
    v_tile = (first_tile[v_e] + vis - v_start[v_e]).astype(I32)
    v_lo = (jnp.maximum(row_start[v_e], v_tile * tm) - v_tile * tm).astype(I32)
    v_hi = (jnp.minimum(row_end[v_e], (v_tile + 1) * tm) - v_tile * tm).astype(I32)
    eid = jnp.arange(N_EXPERTS, dtype=I32)[:, None]
    start_of = lambda e: jnp.sum(jnp.where(e.astype(I32)[None, :] == eid, row_start[:, None], 0), axis=0)
    pos1 = start_of(info[INFO_E1]) + info[INFO_R1].astype(I32)
    pos2 = start_of(info[INFO_E2]) + info[INFO_R2].astype(I32)

    xs = pl.pallas_call(
        _dispatch_kernel,
        grid_spec=pltpu.PrefetchScalarGridSpec(
            num_scalar_prefetch=2, grid=(t // DISPATCH_TILE,),
            in_specs=[pl.BlockSpec(memory_space=pl.ANY)],
            out_specs=pl.BlockSpec(memory_space=pl.ANY),
            scratch_shapes=[pltpu.VMEM((DISPATCH_SLOTS, DISPATCH_TILE * slab, LANES), F32),
                            pltpu.SemaphoreType.DMA((DISPATCH_SLOTS,)),
                            pltpu.SemaphoreType.DMA((DISPATCH_SLOTS,))],
        ),
        out_shape=jax.ShapeDtypeStruct((n_rows, slab, LANES), F32),
        compiler_params=_params("arbitrary"),
        name="moe_dispatch",
    )(pos1, pos2, hn)

    by_tile = lambda v, ve, vt, lo, hi, nv: (vt[v], 0)
    by_expert = lambda v, ve, vt, lo, hi, nv: (layer, ve[v], 0, 0)
    ys = pl.pallas_call(
        _expert_kernel,
        grid_spec=pltpu.PrefetchScalarGridSpec(
            num_scalar_prefetch=5, grid=(n_visits,),
            in_specs=[
                pl.BlockSpec((tm * slab, LANES), by_tile),
                pl.BlockSpec((1, 1, d, f), by_expert),
                pl.BlockSpec((1, 1, d, f), by_expert),
                pl.BlockSpec((1, 1, f, d), by_expert),
            ],
            out_specs=pl.BlockSpec((tm * slab, LANES), by_tile),
            scratch_shapes=[pltpu.VMEM((d, 2 * f), BF16), pltpu.VMEM((f, d), BF16),
                            pltpu.VMEM((2, tm // 2, 2 * f), F32)],
        ),
        out_shape=jax.ShapeDtypeStruct((n_rows * slab, LANES), F32),
        compiler_params=_params("arbitrary"),
        name="moe_experts",
    )(v_e, v_tile, v_lo, v_hi, n_v, xs.reshape(n_rows * slab, LANES), w_gate, w_up, w_down)

    tc = COMBINE_TILE
    out = pl.pallas_call(
        functools.partial(_combine_kernel, final=final),
        grid_spec=pltpu.PrefetchScalarGridSpec(
            num_scalar_prefetch=2, grid=(t // tc,),
            in_specs=[
                pl.BlockSpec((tc, d), lambda i, p1, p2: (i, 0)),
                pl.BlockSpec((SUBLANES, tc), lambda i, p1, p2: (0, i)),
                pl.BlockSpec((1, d), lambda i, p1, p2: (0, 0)),
                pl.BlockSpec(memory_space=pl.ANY),
                pl.BlockSpec(memory_space=pl.ANY),
            ],
            out_specs=pl.BlockSpec((tc, d), lambda i, p1, p2: (i, 0)),
            scratch_shapes=[pltpu.VMEM((2, TOP_K, tc * slab, LANES), F32), pltpu.SemaphoreType.DMA((2,))],
        ),
        out_shape=jax.ShapeDtypeStruct((t, d), F32),
        compiler_params=_params("arbitrary"),
        name="moe_combine",
    )(pos1, pos2, h2, info, final_norm.reshape(1, d), ys.reshape(n_rows, slab, LANES), ys)
    return out.reshape(b, s, d)


MLA_TILE = 256
VT_ROWS = V_HEAD_DIM + 16


def _mla_proj_kernel(h_ref, g_ref, wdn_ref, qn_ref, wuq_ref, kvn_ref, wuk_ref, wuvt_ref,
                     tc_ref, ta_ref, tb_ref, q_ref, k_ref, vt_ref):
    hn = _rms(h_ref[0], g_ref[...]).astype(BF16)
    dn = _dot(hn, wdn_ref[...])
    c, a, b = tc_ref[...], ta_ref[...], tb_ref[...]
    scale = LOG2_E / math.sqrt(QK_NOPE_DIM + QK_ROPE_DIM)
    kv_lo = Q_LORA_RANK
    kr_lo = Q_LORA_RANK + KV_LORA_RANK
    cq = _rms(dn[:, :kv_lo], qn_ref[...]).astype(BF16)
    q = _dot(cq, wuq_ref[...])
    for hh in range(MLA_HEADS):
        sl = slice(hh * HEAD_PAD, (hh + 1) * HEAD_PAD)
        q_ref[0, :, sl] = (_rope_apply(q[:, sl], c, a, b) * scale).astype(BF16)
    ckv = _rms(dn[:, kv_lo:kr_lo], kvn_ref[...]).astype(BF16)
    kn = _dot(ckv, wuk_ref[...])
    kr = _rope_apply(dn[:, kr_lo:kr_lo + HEAD_PAD], c, a, b)
    for hh in range(MLA_HEADS):
        sl = slice(hh * HEAD_PAD, (hh + 1) * HEAD_PAD)
        k_ref[0, :, sl] = (kn[:, sl] + kr).astype(BF16)
    vt = _dot_nt(wuvt_ref[...], ckv).astype(BF16)
    ones = jnp.ones((VT_ROWS - V_HEAD_DIM, vt.shape[1]), BF16)
    for hh in range(MLA_HEADS):
        vt_ref[0, hh, 0, 0:V_HEAD_DIM, :] = vt[hh * V_HEAD_DIM:(hh + 1) * V_HEAD_DIM, :]
        vt_ref[0, hh, 0, V_HEAD_DIM:VT_ROWS, :] = ones


def _mla_proj(h, g, w_down, q_norm, w_uq, kv_norm, w_ukv, tabs):
    b, s, d = h.shape
    ts = MLA_TILE
    nh, dn_, dr, dv = MLA_HEADS, QK_NOPE_DIM, QK_ROPE_DIM, V_HEAD_DIM
    kr_lo = Q_LORA_RANK + KV_LORA_RANK
    w_dn = jnp.concatenate([
        w_down[:, :kr_lo], jnp.zeros((d, dn_), F32), w_down[:, kr_lo:],
        jnp.zeros((d, HEAD_PAD - dn_ - dr), F32)], axis=1).astype(BF16)
    wq = w_uq.reshape(Q_LORA_RANK, nh, dn_ + dr)
    wq = jnp.pad(wq, ((0, 0), (0, 0), (0, HEAD_PAD - dn_ - dr))).reshape(Q_LORA_RANK, nh * HEAD_PAD).astype(BF16)
    wkv = w_ukv.reshape(KV_LORA_RANK, nh, dn_ + dv)
    wk = jnp.pad(wkv[:, :, :dn_], ((0, 0), (0, 0), (0, HEAD_PAD - dn_))).reshape(KV_LORA_RANK, nh * HEAD_PAD).astype(BF16)
    wvt = wkv[:, :, dn_:].reshape(KV_LORA_RANK, nh * dv).T.astype(BF16)
    tab_c, tab_a, tab_b = tabs
    nblk = s // ts
    tab = pl.BlockSpec((ts, HEAD_PAD), lambda i, j: (i * nblk + j, 0))
    tile = lambda w: pl.BlockSpec((1, ts, w), lambda i, j: (i, j, 0))
    return pl.pallas_call(
        _mla_proj_kernel,
        grid=(b, nblk),
        in_specs=[tile(d), _const_spec((1, d)), _const_spec(w_dn.shape), _const_spec((1, Q_LORA_RANK)),
                  _const_spec(wq.shape), _const_spec((1, KV_LORA_RANK)), _const_spec(wk.shape),
                  _const_spec(wvt.shape), tab, tab, tab],
        out_specs=(tile(nh * HEAD_PAD), tile(nh * HEAD_PAD),
                   pl.BlockSpec((1, nh, 1, VT_ROWS, ts), lambda i, j: (i, 0, j, 0, 0))),
        out_shape=(jax.ShapeDtypeStruct((b, s, nh * HEAD_PAD), BF16),
                   jax.ShapeDtypeStruct((b, s, nh * HEAD_PAD), BF16),
                   jax.ShapeDtypeStruct((b, nh, nblk, VT_ROWS, ts), BF16)),
        compiler_params=_params("arbitrary", "arbitrary"),
        name="mla_proj",
    )(h, g.reshape(1, d), w_dn, q_norm.reshape(1, -1), wq, kv_norm.reshape(1, -1), wk, wvt,
      tab_c, tab_a, tab_b)


def _flash_kernel(q_ref, k_ref, vt_ref, o_ref, sa_ref, sb_ref, acc_ref):
    tq = q_ref.shape[1]
    tk = vt_ref.shape[4]
    qi = pl.program_id(2)

    def scores(j, s_ref, c0=0):
        k2 = k_ref[0, pl.ds(pl.multiple_of(j * tk, tk), tk), :]
        for hh in range(2):
            sl = slice(hh * HEAD_PAD, (hh + 1) * HEAD_PAD)
            s_ref[hh, :, c0:] = _dot_nt(k2[:, sl], q_ref[0, c0:, sl])

    def consume(j, s_ref, ms, c0=0, diag=False):
        out = []
        for hh in range(2):
            st = s_ref[hh, :, c0:]
            if diag:
                key = lax.broadcasted_iota(I32, st.shape, 0)
                qry = lax.broadcasted_iota(I32, st.shape, 1)
                st = jnp.where(key <= qry, st, NEG_BIG)
            m = ms[hh][:, c0:]
            m_new = jnp.maximum(m, jnp.max(st, axis=0, keepdims=True))
            alpha = jnp.exp2(m - m_new)
            p = jnp.exp2(st - m_new).astype(BF16)
            acc_ref[hh, :, c0:] = alpha * acc_ref[hh, :, c0:] + _dot(vt_ref[0, hh, j], p)
            out.append(m_new if c0 == 0 else jnp.concatenate([ms[hh][:, :c0], m_new], axis=1))
        return tuple(out)

    acc_ref[...] = jnp.zeros_like(acc_ref)
    ms = tuple(jnp.full((1, tq), NEG_BIG, F32) for _ in range(2))
    scores(0, sa_ref)

    def two_tiles(i, ms):
        scores(2 * i + 1, sb_ref)
        ms = consume(2 * i, sa_ref, ms)
        scores(2 * i + 2, sa_ref)
        return consume(2 * i + 1, sb_ref, ms)

    assert tq == 2 * tk
    ms = lax.fori_loop(0, qi, two_tiles, ms)
    scores(2 * qi + 1, sb_ref, c0=tk)
    ms = consume(2 * qi, sa_ref, ms, diag=True)
    consume(2 * qi + 1, sb_ref, ms, c0=tk, diag=True)
    heads = [acc_ref[hh, 0:V_HEAD_DIM, :] * (1.0 / acc_ref[hh, V_HEAD_DIM:V_HEAD_DIM + 1, :]) for hh in range(2)]
    o_ref[0] = jnp.concatenate(heads, axis=0).T.astype(BF16)


def _flash(q, k, vt, tq=512):
    b, s, _ = q.shape
    _, nh, nblk, rows, tk = vt.shape
    return pl.pallas_call(
        _flash_kernel,
        grid=(b, nh // 2, s // tq),
        in_specs=[
            pl.BlockSpec((1, tq, 2 * HEAD_PAD), lambda i, p, j: (i, j, p)),
            pl.BlockSpec((1, s, 2 * HEAD_PAD), lambda i, p, j: (i, 0, p)),
            pl.BlockSpec((1, 2, nblk, rows, tk), lambda i, p, j: (i, p, 0, 0, 0)),
        ],
        out_specs=pl.BlockSpec((1, tq, 2 * V_HEAD_DIM), lambda i, p, j: (i, j, p)),
        out_shape=jax.ShapeDtypeStruct((b, s, nh * V_HEAD_DIM), BF16),
        scratch_shapes=[pltpu.VMEM((2, tk, tq), F32), pltpu.VMEM((2, tk, tq), F32),
                        pltpu.VMEM((2, rows, tq), F32)],
        compiler_params=_params("arbitrary", "arbitrary", "arbitrary"),
        name="flash",
    )(q, k, vt)


def kernel(x, mem, positions, norm_mix, conv_w_in, conv_b_in, conv_w_dw, conv_b_dw, conv_ln_g, conv_ln_b, conv_w_out, conv_b_out, mla_w_down, mla_q_norm, mla_w_uq, mla_kv_norm, mla_w_ukv, mla_w_o, norm_xa, norm_mem, xa_wq, xa_wkv, xa_wo, norm_ffn, moe_w_grp, moe_b_grp, moe_w_exp, moe_b_exp, moe_w_gate, moe_w_up, moe_w_down, final_norm):
    tabs = _rope_tables(positions)
    kv_mem = _mem_kv(mem, norm_mem, xa_wkv.astype(BF16))

    def ffn(h, i, final):
        return _moe(h, norm_ffn[i], moe_w_grp[i], moe_b_grp[i], moe_w_exp[i], moe_b_exp[i],
                    moe_w_gate, moe_w_up, moe_w_down, i, final_norm, final)

    h = _conv_mixer(x, norm_mix[0], conv_w_in[0], conv_b_in[0], conv_w_dw[0], conv_b_dw[0],
                    conv_ln_g[0], conv_ln_b[0], conv_w_out[0], conv_b_out[0])
    h = _xattn(h, norm_xa[0], xa_wq[0], kv_mem, 0, xa_wo[0])
    h = ffn(h, 0, False)
    q, k, vt = _mla_proj(h, norm_mix[1], mla_w_down[0], mla_q_norm[0], mla_w_uq[0], mla_kv_norm[0],
                         mla_w_ukv[0], tabs)
    attn = _flash(q, k, vt)
    h = _xattn(h, norm_xa[1], xa_wq[1], kv_mem, 1, xa_wo[1], prefix=(attn, mla_w_o[0]))
    return ffn(h, 1, True)
```

```python
import functools
import math

import jax
import jax.numpy as jnp
from jax import lax
from jax.experimental import pallas as pl
from jax.experimental.pallas import tpu as pltpu

F32 = jnp.float32
BF16 = jnp.bfloat16
I32 = jnp.int32

NORM_EPS = 1e-6
CONV_WIDTH = 31
CONV_HALO = 32
SUBLANES = 8
MLA_HEADS = 16
QK_NOPE_DIM = 64
QK_ROPE_DIM = 32
V_HEAD_DIM = 64
Q_LORA_RANK = 384
KV_LORA_RANK = 256
ROPE_THETA = 10000.0
HEAD_PAD = 128
XA_HEADS = 4
N_GROUPS = 4
EXPERTS_PER_GROUP = 8
N_EXPERTS = N_GROUPS * EXPERTS_PER_GROUP
TOP_K = 2
ROUTER_LANES = 128
NEG_BIG = -1e30
LOG2_E = 1.4426950408889634

VMEM_LIMIT = 56 * 1024 * 1024


def _rms(x, g):
    return x * lax.rsqrt(jnp.mean(x * x, axis=-1, keepdims=True) + NORM_EPS) * g


def _dot(a, b):
    return jnp.dot(a, b, preferred_element_type=F32)


def _dot_nt(a, b):
    return lax.dot_general(a, b, (((1,), (1,)), ((), ())), preferred_element_type=F32)


def _params(*sem):
    return pltpu.CompilerParams(dimension_semantics=sem, vmem_limit_bytes=VMEM_LIMIT)


def _const_spec(shape):
    return pl.BlockSpec(shape, lambda *_: (0,) * len(shape))


def _rope_kernel(pos_ref, freq_ref, cos_ref, sin_ref):
    ang = pos_ref[...].astype(F32) * freq_ref[...]
    cos_ref[...] = jnp.cos(ang)
    sin_ref[...] = jnp.sin(ang)


def _rope_tables(positions):
    half = QK_ROPE_DIM // 2
    t = positions.size
    inv_freq = 1.0 / (ROPE_THETA ** (jnp.arange(0, QK_ROPE_DIM, 2, dtype=F32) / QK_ROPE_DIM))
    rows = t * half // 128
    pos_rep = jnp.repeat(positions.reshape(-1), half).reshape(rows, 128)
    freq = jnp.tile(inv_freq, 128 // half).reshape(1, 128)
    cos, sin = pl.pallas_call(
        _rope_kernel,
        out_shape=(jax.ShapeDtypeStruct((rows, 128), F32),) * 2,
        name="rope_tables",
    )(pos_rep, freq)
    cos = cos.reshape(t, half)
    sin = sin.reshape(t, half)
    one = jnp.ones((t, QK_NOPE_DIM), F32)
    z16 = jnp.zeros((t, half), F32)
    z64 = jnp.zeros((t, QK_NOPE_DIM), F32)
    pad1 = jnp.ones((t, HEAD_PAD - QK_NOPE_DIM - QK_ROPE_DIM), F32)
    pad0 = jnp.zeros((t, HEAD_PAD - QK_NOPE_DIM - QK_ROPE_DIM), F32)
    tab_c = jnp.concatenate([one, cos, cos, pad1], axis=1)
    tab_a = jnp.concatenate([z64, sin, z16, pad0], axis=1)
    tab_b = jnp.concatenate([z64, z16, sin, pad0], axis=1)
    return tab_c, tab_a, tab_b


def _rope_apply(x, c, a, b):
    return x * c - pltpu.roll(x, HEAD_PAD - 16, 1) * a + pltpu.roll(x, 16, 1) * b


def _mem_kv_kernel(mem_ref, g_ref, w_ref, o_ref):
    mn = _rms(mem_ref[0], g_ref[0]).astype(BF16)
    o_ref[0, 0] = _dot(mn, w_ref[0]).astype(BF16)


def _mem_kv(mem, norm_mem, wkv):
    b, m, d = mem.shape
    l = norm_mem.shape[0]
    return pl.pallas_call(
        _mem_kv_kernel,
        grid=(l, b),
        in_specs=[
            pl.BlockSpec((1, m, d), lambda i, j: (j, 0, 0)),
            pl.BlockSpec((1, 1, d), lambda i, j: (i, 0, 0)),
            pl.BlockSpec((1, d, 2 * d), lambda i, j: (i, 0, 0)),
        ],
        out_specs=pl.BlockSpec((1, 1, m, 2 * d), lambda i, j: (i, j, 0, 0)),
        out_shape=jax.ShapeDtypeStruct((l, b, m, 2 * d), BF16),
        compiler_params=_params("arbitrary", "arbitrary"),
        name="mem_kv",
    )(mem, norm_mem.reshape(l, 1, d), wkv)


CONV_ROWS = 16
CONV_COLS = 4


def _conv_mixer_kernel(h_ref, g_ref, win_ref, bin_ref, wdw_ref, bdw_ref, lng_ref, lnb_ref,
                       wout_ref, bout_ref, o_ref, cb_ref, cv_ref):
    ts, d = h_ref.shape[1], h_ref.shape[2]

    @pl.when(pl.program_id(1) == 0)
    def _():
        cb_ref[0, 0:CONV_HALO, :] = jnp.zeros((CONV_HALO, d), F32)

    h = h_ref[0]
    hn = _rms(h, g_ref[...]).astype(BF16)
    u = _dot(hn, win_ref[...]) + bin_ref[...]
    cb_ref[0, CONV_HALO:CONV_HALO + ts, :] = u[:, :d] * jax.nn.sigmoid(u[:, d:])
    ncopy = ts + CONV_HALO - SUBLANES
    for s in range(1, SUBLANES):
        cb_ref[s, 0:ncopy, :] = cb_ref[0, s:s + ncopy, :]

    base = CONV_HALO - (CONV_WIDTH - 1)

    def rows(i, carry):
        r0 = pl.multiple_of(i * CONV_ROWS, CONV_ROWS)
        for cg in range(0, d, LANES * CONV_COLS):
            cols = [cg + LANES * c for c in range(CONV_COLS)]
            accs = [jnp.zeros((CONV_ROWS, LANES), F32) for _ in cols]
            for k in range(CONV_WIDTH):
                off = base + k
                row0 = pl.multiple_of(r0 + (off // SUBLANES) * SUBLANES, SUBLANES)
                for ci, c0 in enumerate(cols):
                    tap = cb_ref[off % SUBLANES, pl.ds(row0, CONV_ROWS), c0:c0 + LANES]
                    accs[ci] = accs[ci] + tap * wdw_ref[k:k + 1, c0:c0 + LANES]
            for ci, c0 in enumerate(cols):
                cv_ref[pl.ds(r0, CONV_ROWS), c0:c0 + LANES] = accs[ci]
        return carry

    lax.fori_loop(0, ts // CONV_ROWS, rows, 0)
    cb_ref[0, 0:CONV_HALO, :] = cb_ref[0, ts:ts + CONV_HALO, :]

    v = cv_ref[...] + bdw_ref[...]
    mu = jnp.mean(v, axis=-1, keepdims=True)
    vc = v - mu
    var = jnp.mean(vc * vc, axis=-1, keepdims=True)
    y = vc * lax.rsqrt(var + NORM_EPS) * lng_ref[...] + lnb_ref[...]
    y = (y * jax.nn.sigmoid(y)).astype(BF16)
    o_ref[0] = h + _dot(y, wout_ref[...]) + bout_ref[...]


def _conv_mixer(h, g, w_in, b_in, w_dw, b_dw, ln_g, ln_b, w_out, b_out, ts=256):
    b, s, d = h.shape
    row = lambda v: v.reshape(1, -1)
    return pl.pallas_call(
        _conv_mixer_kernel,
        grid=(b, s // ts),
        in_specs=[
            pl.BlockSpec((1, ts, d), lambda i, j: (i, j, 0)),
            _const_spec((1, d)), _const_spec((d, 2 * d)), _const_spec((1, 2 * d)),
            _const_spec((CONV_WIDTH, d)), _const_spec((1, d)), _const_spec((1, d)), _const_spec((1, d)),
            _const_spec((d, d)), _const_spec((1, d)),
        ],
        out_specs=pl.BlockSpec((1, ts, d), lambda i, j: (i, j, 0)),
        out_shape=jax.ShapeDtypeStruct((b, s, d), F32),
        scratch_shapes=[pltpu.VMEM((SUBLANES, ts + CONV_HALO, d), F32), pltpu.VMEM((ts, d), F32)],
        compiler_params=_params("arbitrary", "arbitrary"),
        name="conv_mixer",
    )(h, row(g), w_in.astype(BF16), row(b_in), w_dw, row(b_dw), row(ln_g), row(ln_b),
      w_out.astype(BF16), row(b_out))


def _xattn_kernel(*refs, with_prefix):
    if with_prefix:
        h_ref, a_ref, wmo_ref, g_ref, wq_ref, kv_ref, wo_ref, o_ref = refs
        h = h_ref[0] + _dot(a_ref[0], wmo_ref[...])
    else:
        h_ref, g_ref, wq_ref, kv_ref, wo_ref, o_ref = refs
        h = h_ref[0]
    d = h.shape[-1]
    hd = d // XA_HEADS
    hn = _rms(h, g_ref[...]).astype(BF16)
    q = (_dot(hn, wq_ref[...]) * (1.0 / math.sqrt(hd))).astype(BF16)
    outs = []
    for hh in range(XA_HEADS):
        k = kv_ref[0, 0, :, hh * hd:(hh + 1) * hd]
        v = kv_ref[0, 0, :, d + hh * hd:d + (hh + 1) * hd]
        sc = _dot_nt(q[:, hh * hd:(hh + 1) * hd], k)
        p = jnp.exp(sc - jnp.max(sc, axis=-1, keepdims=True))
        inv = 1.0 / jnp.sum(p, axis=-1, keepdims=True)
        outs.append((_dot(p.astype(BF16), v) * inv).astype(BF16))
    o = jnp.concatenate(outs, axis=-1)
    o_ref[0] = h + _dot(o, wo_ref[...])


def _xattn(h, g, wq, kv, layer, wo, prefix=None, ts=512):
    b, s, d = h.shape
    m = kv.shape[2]
    tile = pl.BlockSpec((1, ts, d), lambda i, j: (i, j, 0))
    in_specs = [tile]
    args = [h]
    if prefix is not None:
        attn, w_mo = prefix
        in_specs += [tile, _const_spec((d, d))]
        args += [attn, w_mo.astype(BF16)]
    in_specs += [_const_spec((1, d)), _const_spec((d, d)),
                 pl.BlockSpec((1, 1, m, 2 * d), lambda i, j: (layer, i, 0, 0)), _const_spec((d, d))]
    args += [g.reshape(1, d), wq.astype(BF16), kv, wo.astype(BF16)]
    return pl.pallas_call(
        functools.partial(_xattn_kernel, with_prefix=prefix is not None),
        grid=(b, s // ts),
        in_specs=in_specs,
        out_specs=tile,
        out_shape=jax.ShapeDtypeStruct((b, s, d), F32),
        compiler_params=_params("arbitrary", "arbitrary"),
        name="xattn",
    )(*args)


ROUTE_TILE = 512
EXPERT_TILE = 256
COMBINE_TILE = 256
DISPATCH_TILE = 256
LANES = 128


def _load_slabs(ref, rows, d):
    n = d // LANES
    return jnp.concatenate([ref[pl.ds(j, rows, stride=n), :] for j in range(n)], axis=1)


def _store_slabs(ref, val, accumulate=False):
    rows, d = val.shape
    n = d // LANES
    for j in range(n):
        chunk = val[:, j * LANES:(j + 1) * LANES]
        if accumulate:
            chunk = chunk + ref[pl.ds(j, rows, stride=n), :]
        ref[pl.ds(j, rows, stride=n), :] = chunk


INFO_E1, INFO_E2, INFO_R1, INFO_R2, INFO_W1, INFO_W2 = range(6)
ROUTER_EXPERT_ROW = SUBLANES
ROUTER_ROWS = ROUTER_EXPERT_ROW + N_EXPERTS + SUBLANES


def _router_kernel(h_ref, g_ref, wrt_ref, brt_ref, hn_ref, info_ref, cnt_ref, carry_ref):
    tm = h_ref.shape[0]
    assert EXPERTS_PER_GROUP == SUBLANES

    @pl.when(pl.program_id(0) == 0)
    def _():
        carry_ref[...] = jnp.zeros_like(carry_ref)

    hn = _rms(h_ref[...], g_ref[...])
    _store_slabs(hn_ref, hn)
    logits = lax.dot_general(wrt_ref[...], hn, (((1,), (1,)), ((), ())), preferred_element_type=F32,
                             precision=lax.Precision.HIGHEST) + brt_ref[:, 0:1]
    sub = lax.broadcasted_iota(I32, (SUBLANES, tm), 0)
    gl = jnp.where(sub < N_GROUPS, logits[0:SUBLANES], NEG_BIG)
    gmax = jnp.max(gl, axis=0, keepdims=True)
    gsum = jnp.sum(jnp.exp(gl - gmax), axis=0, keepdims=True)
    g_val = 1.0 / gsum
    g_idx = jnp.min(jnp.where(gl == gmax, sub, SUBLANES), axis=0, keepdims=True)
    el = logits[ROUTER_EXPERT_ROW:ROUTER_EXPERT_ROW + SUBLANES]
    for grp in range(1, N_GROUPS):
        lo = ROUTER_EXPERT_ROW + grp * SUBLANES
        el = jnp.where(g_idx == grp, logits[lo:lo + SUBLANES], el)
    m1 = jnp.max(el, axis=0, keepdims=True)
    esum = jnp.sum(jnp.exp(el - m1), axis=0, keepdims=True)
    i1 = jnp.min(jnp.where(el == m1, sub, SUBLANES), axis=0, keepdims=True)
    el2 = jnp.where(sub == i1, NEG_BIG, el)
    m2 = jnp.max(el2, axis=0, keepdims=True)
    i2 = jnp.min(jnp.where(el2 == m2, sub, SUBLANES), axis=0, keepdims=True)
    p1 = 1.0 / esum
    p2 = jnp.exp(m2 - m1) / esum
    w1 = g_val * (p1 / (p1 + p2))
    w2 = g_val * (p2 / (p1 + p2))
    e1 = g_idx * EXPERTS_PER_GROUP + i1
    e2 = g_idx * EXPERTS_PER_GROUP + i2
    erow = lax.broadcasted_iota(I32, (N_EXPERTS, tm), 0)
    hit = jnp.where((erow == e1) | (erow == e2), 1.0, 0.0)
    r = lax.broadcasted_iota(I32, (tm, tm), 0)
    c = lax.broadcasted_iota(I32, (tm, tm), 1)
    before = jnp.where(r < c, 1.0, 0.0).astype(BF16)
    tot = _dot(hit.astype(BF16), before) + carry_ref[:, 0:1]
    rank1 = jnp.sum(jnp.where(erow == e1, tot, 0.0), axis=0, keepdims=True)
    rank2 = jnp.sum(jnp.where(erow == e2, tot, 0.0), axis=0, keepdims=True)
    carry_ref[...] = carry_ref[...] + jnp.sum(hit, axis=1, keepdims=True)
    cnt_ref[...] = carry_ref[...]
    info = jnp.zeros((SUBLANES, tm), F32)
    for row, val in ((INFO_E1, e1.astype(F32)), (INFO_E2, e2.astype(F32)), (INFO_R1, rank1),
                     (INFO_R2, rank2), (INFO_W1, w1), (INFO_W2, w2)):
        info = jnp.where(sub == row, val, info)
    info_ref[...] = info


DISPATCH_SLOTS = 3


def _dispatch_kernel(p1_ref, p2_ref, hn_ref, xs_ref, buf, lsem, ssem):
    i = pl.program_id(0)
    n = pl.num_programs(0)
    slab = xs_ref.shape[1]
    rows = buf.shape[1]
    tm = rows // slab
    slot = i % DISPATCH_SLOTS

    def load(step):
        sl = step % DISPATCH_SLOTS
        return pltpu.make_async_copy(hn_ref.at[pl.ds(step * rows, rows)], buf.at[sl], lsem.at[sl])

    def wait_scatters(sl):
        cp = pltpu.make_async_copy(buf.at[sl], hn_ref.at[pl.ds(0, rows)], ssem.at[sl])
        for _ in range(TOP_K):
            cp.wait()

    @pl.when(i == 0)
    def _():
        load(0).start()

    @pl.when(i + 1 < n)
    def _():
        load(i + 1).start()

    load(i).wait()

    def tok(r, carry):
        tt = i * tm + r
        src = buf.at[slot, pl.ds(pl.multiple_of(r * slab, slab), slab)]
        pltpu.make_async_copy(src, xs_ref.at[p1_ref[tt]], ssem.at[slot]).start()
        pltpu.make_async_copy(src, xs_ref.at[p2_ref[tt]], ssem.at[slot]).start()
        return carry

    lax.fori_loop(0, tm, tok, 0, unroll=8)

    @pl.when(i > 0)
    def _():
        wait_scatters((i - 1) % DISPATCH_SLOTS)

    @pl.when(i == n - 1)
    def _():
        wait_scatters(slot)


def _expert_kernel(ve_ref, vt_ref, lo_ref, hi_ref, nv_ref, x_ref, wg_ref, wu_ref, wd_ref, y_ref,
                   wgu_s, wd_s, gu_s):
    v = pl.program_id(0)
    d, f = wg_ref.shape[2], wg_ref.shape[3]
    tm = x_ref.shape[0] * LANES // d

    @pl.when(v < nv_ref[0])
    def _():
        pv = jnp.maximum(v - 1, 0)

        @pl.when((v == 0) | (ve_ref[v] != ve_ref[pv]))
        def _():
            wgu_s[:, :f] = wg_ref[0, 0].astype(BF16)
            wgu_s[:, f:] = wu_ref[0, 0].astype(BF16)
            wd_s[...] = wd_ref[0, 0].astype(BF16)

        half = tm // 2
        slab = d // LANES

        @pl.when((v == 0) | (vt_ref[v] != vt_ref[pv]))
        def _():
            y_ref[...] = jnp.zeros_like(y_ref)

        for hh in range(2):
            x = _load_slabs(x_ref.at[pl.ds(hh * half * slab, half * slab)], half, d)
            gu_s[hh] = _dot(x.astype(BF16), wgu_s[...])
        for hh in range(2):
            gate = gu_s[hh, :, :f]
            he = (gate * jax.nn.sigmoid(gate) * gu_s[hh, :, f:]).astype(BF16)
            row = lax.broadcasted_iota(I32, (half, 1), 0) + hh * half
            y = jnp.where((row >= lo_ref[v]) & (row < hi_ref[v]), _dot(he, wd_s[...]), 0.0)
            _store_slabs(y_ref.at[pl.ds(hh * half * slab, half * slab)], y, accumulate=True)


def _combine_kernel(p1_ref, p2_ref, h_ref, info_ref, fin_ref, ys_ref, ys2d_ref, *rest, final, with_mla):
    if with_mla:
        mla_in, (o_ref, q_ref, k_ref, vt_ref, buf, sem) = rest[:10], rest[10:]
    else:
        o_ref, buf, sem = rest
    i = pl.program_id(0)
    n = pl.num_programs(0)
    tm, d = h_ref.shape
    slab = ys_ref.shape[1]
    slot = i % 2

    def issue(step, sl):
        def tok(r, carry):
            tt = step * tm + r
            dst = pl.ds(pl.multiple_of(r * slab, slab), slab)
            pltpu.make_async_copy(ys_ref.at[p1_ref[tt]], buf.at[sl, 0, dst], sem.at[sl]).start()
            pltpu.make_async_copy(ys_ref.at[p2_ref[tt]], buf.at[sl, 1, dst], sem.at[sl]).start()
            return carry

        lax.fori_loop(0, tm, tok, 0, unroll=8)

    @pl.when(i == 0)
    def _():
        issue(0, 0)

    @pl.when(i + 1 < n)
    def _():
        issue(i + 1, 1 - slot)

    for k in range(TOP_K):
        pltpu.make_async_copy(ys2d_ref.at[pl.ds(0, tm * slab)], buf.at[slot, k], sem.at[slot]).wait()
    info = jnp.concatenate([info_ref[...], jnp.zeros((ROUTER_LANES - SUBLANES, tm), F32)], axis=0).T
    out = (h_ref[...] + info[:, INFO_W1:INFO_W1 + 1] * _load_slabs(buf.at[slot, 0], tm, d)
           + info[:, INFO_W2:INFO_W2 + 1] * _load_slabs(buf.at[slot, 1], tm, d))
    if final:
        out = _rms(out, fin_ref[...])
    o_ref[...] = out
    if with_mla:
        _mla_tile(out, *mla_in, q_ref, k_ref, vt_ref)


def _moe(h, g, w_grp, b_grp, w_exp, b_exp, w_gate, w_up, w_down, layer, final_norm, final, mla=None):
    b, s, d = h.shape
    t = b * s
    f = w_gate.shape[-1]
    slab = d // LANES
    assert slab == SUBLANES
    h2 = h.reshape(t, d)
    zrow = lambda n: jnp.zeros((n, d), F32)
    wrt = jnp.concatenate([w_grp.T, zrow(ROUTER_EXPERT_ROW - N_GROUPS), w_exp.T,
                           zrow(ROUTER_ROWS - ROUTER_EXPERT_ROW - N_EXPERTS)], axis=0)
    brt = jnp.concatenate([b_grp, jnp.zeros((ROUTER_EXPERT_ROW - N_GROUPS,), F32), b_exp,
                           jnp.zeros((ROUTER_ROWS - ROUTER_EXPERT_ROW - N_EXPERTS,), F32)])
    brt = jnp.broadcast_to(brt[:, None], (ROUTER_ROWS, ROUTER_LANES))

    hn, info, cnt = pl.pallas_call(
        _router_kernel,
        grid=(t // ROUTE_TILE,),
        in_specs=[pl.BlockSpec((ROUTE_TILE, d), lambda i: (i, 0)), _const_spec((1, d)),
                  _const_spec((ROUTER_ROWS, d)), _const_spec((ROUTER_ROWS, ROUTER_LANES))],
        out_specs=(pl.BlockSpec((ROUTE_TILE * slab, LANES), lambda i: (i, 0)),
                   pl.BlockSpec((SUBLANES, ROUTE_TILE), lambda i: (0, i)),
                   _const_spec((N_EXPERTS, ROUTER_LANES))),
        out_shape=(jax.ShapeDtypeStruct((t * slab, LANES), F32), jax.ShapeDtypeStruct((SUBLANES, t), F32),
                   jax.ShapeDtypeStruct((N_EXPERTS, ROUTER_LANES), F32)),
        scratch_shapes=[pltpu.VMEM((N_EXPERTS, ROUTER_LANES), F32)],
        compiler_params=_params("arbitrary"),
        name="moe_router",
    )(h2, g.reshape(1, d), wrt, brt)

    tm = EXPERT_TILE
    n_rows = TOP_K * t
    n_visits = n_rows // tm + N_EXPERTS - 1
    counts = cnt[:, 0].astype(I32)
    row_end = jnp.cumsum(counts)
    row_start = row_end - counts
    first_tile = row_start // tm
    e_visits = jnp.where(counts > 0, (row_end - 1) // tm - first_tile + 1, 0)
    v_end = jnp.cumsum(e_visits)
    v_start = v_end - e_visits
    n_v = v_end[-1:]
    vis = jnp.minimum(jnp.arange(n_visits, dtype=I32), n_v - 1)
    v_e = jnp.sum(vis[:, None] >= v_end[None, :], axis=1).astype(I32)
    v_tile = (first_tile[v_e] + vis - v_start[v_e]).astype(I32)
    v_lo = (jnp.maximum(row_start[v_e], v_tile * tm) - v_tile * tm).astype(I32)
    v_hi = (jnp.minimum(row_end[v_e], (v_tile + 1) * tm) - v_tile * tm).astype(I32)
    eid = jnp.arange(N_EXPERTS, dtype=I32)[:, None]
    start_of = lambda e: jnp.sum(jnp.where(e.astype(I32)[None, :] == eid, row_start[:, None], 0), axis=0)
    pos1 = start_of(info[INFO_E1]) + info[INFO_R1].astype(I32)
    pos2 = start_of(info[INFO_E2]) + info[INFO_R2].astype(I32)

    xs = pl.pallas_call(
        _dispatch_kernel,
        grid_spec=pltpu.PrefetchScalarGridSpec(
            num_scalar_prefetch=2, grid=(t // DISPATCH_TILE,),
            in_specs=[pl.BlockSpec(memory_space=pl.ANY)],
            out_specs=pl.BlockSpec(memory_space=pl.ANY),
            scratch_shapes=[pltpu.VMEM((DISPATCH_SLOTS, DISPATCH_TILE * slab, LANES), F32),
                            pltpu.SemaphoreType.DMA((DISPATCH_SLOTS,)),
                            pltpu.SemaphoreType.DMA((DISPATCH_SLOTS,))],
        ),
        out_shape=jax.ShapeDtypeStruct((n_rows, slab, LANES), F32),
        compiler_params=_params("arbitrary"),
        name="moe_dispatch",
    )(pos1, pos2, hn)

    by_tile = lambda v, ve, vt, lo, hi, nv: (vt[v], 0)
    by_expert = lambda v, ve, vt, lo, hi, nv: (layer, ve[v], 0, 0)
    ys = pl.pallas_call(
        _expert_kernel,
        grid_spec=pltpu.PrefetchScalarGridSpec(
            num_scalar_prefetch=5, grid=(n_visits,),
            in_specs=[
                pl.BlockSpec((tm * slab, LANES), by_tile),
                pl.BlockSpec((1, 1, d, f), by_expert),
                pl.BlockSpec((1, 1, d, f), by_expert),
                pl.BlockSpec((1, 1, f, d), by_expert),
            ],
            out_specs=pl.BlockSpec((tm * slab, LANES), by_tile),
            scratch_shapes=[pltpu.VMEM((d, 2 * f), BF16), pltpu.VMEM((f, d), BF16),
                            pltpu.VMEM((2, tm // 2, 2 * f), F32)],
        ),
        out_shape=jax.ShapeDtypeStruct((n_rows * slab, LANES), F32),
        compiler_params=_params("arbitrary"),
        name="moe_experts",
    )(v_e, v_tile, v_lo, v_hi, n_v, xs.reshape(n_rows * slab, LANES), w_gate, w_up, w_down)

    tc = COMBINE_TILE
    tok = lambda w: pl.BlockSpec((tc, w), lambda i, p1, p2: (i, 0))
    in_specs = [tok(d), pl.BlockSpec((SUBLANES, tc), lambda i, p1, p2: (0, i)),
                pl.BlockSpec((1, d), lambda i, p1, p2: (0, 0)),
                pl.BlockSpec(memory_space=pl.ANY), pl.BlockSpec(memory_space=pl.ANY)]
    args = [h2, info, final_norm.reshape(1, d), ys.reshape(n_rows, slab, LANES), ys]
    out_specs = tok(d)
    out_shape = jax.ShapeDtypeStruct((t, d), F32)
    if mla is not None:
        assert tc == MLA_TILE and s % tc == 0
        nblk = s // tc
        width = MLA_HEADS * HEAD_PAD
        in_specs += [pl.BlockSpec(w.shape, lambda i, p1, p2, nd=w.ndim: (0,) * nd) for w in mla[:7]]
        in_specs += [tok(HEAD_PAD)] * 3
        args += list(mla)
        out_specs = (out_specs, tok(width), tok(width),
                     pl.BlockSpec((1, MLA_HEADS, 1, VT_ROWS, tc), lambda i, p1, p2: (i // nblk, 0, i % nblk, 0, 0)))
        out_shape = (out_shape, jax.ShapeDtypeStruct((t, width), BF16), jax.ShapeDtypeStruct((t, width), BF16),
                     jax.ShapeDtypeStruct((b, MLA_HEADS, nblk, VT_ROWS, tc), BF16))
    res = pl.pallas_call(
        functools.partial(_combine_kernel, final=final, with_mla=mla is not None),
        grid_spec=pltpu.PrefetchScalarGridSpec(
            num_scalar_prefetch=2, grid=(t // tc,),
            in_specs=in_specs,
            out_specs=out_specs,
            scratch_shapes=[pltpu.VMEM((2, TOP_K, tc * slab, LANES), F32), pltpu.SemaphoreType.DMA((2,))],
        ),
        out_shape=out_shape,
        compiler_params=_params("arbitrary"),
        name="moe_combine",
    )(pos1, pos2, *args)
    if mla is None:
        return res.reshape(b, s, d)
    out, q, k, vt = res
    return out.reshape(b, s, d), q.reshape(b, s, -1), k.reshape(b, s, -1), vt


MLA_TILE = 256
VT_ROWS = V_HEAD_DIM + 16


def _mla_tile(h, g_ref, wdn_ref, qn_ref, wuq_ref, kvn_ref, wuk_ref, wuvt_ref, tc_ref, ta_ref, tb_ref,
              q_ref, k_ref, vt_ref):
    hn = _rms(h, g_ref[...]).astype(BF16)
    dn = _dot(hn, wdn_ref[...])
    c, a, b = tc_ref[...], ta_ref[...], tb_ref[...]
    scale = LOG2_E / math.sqrt(QK_NOPE_DIM + QK_ROPE_DIM)
    kv_lo = Q_LORA_RANK
    kr_lo = Q_LORA_RANK + KV_LORA_RANK
    cq = _rms(dn[:, :kv_lo], qn_ref[...]).astype(BF16)
    q = _dot(cq, wuq_ref[...])
    for hh in range(MLA_HEADS):
        sl = slice(hh * HEAD_PAD, (hh + 1) * HEAD_PAD)
        q_ref[:, sl] = (_rope_apply(q[:, sl], c, a, b) * scale).astype(BF16)
    ckv = _rms(dn[:, kv_lo:kr_lo], kvn_ref[...]).astype(BF16)
    kn = _dot(ckv, wuk_ref[...])
    kr = _rope_apply(dn[:, kr_lo:kr_lo + HEAD_PAD], c, a, b)
    for hh in range(MLA_HEADS):
        sl = slice(hh * HEAD_PAD, (hh + 1) * HEAD_PAD)
        k_ref[:, sl] = (kn[:, sl] + kr).astype(BF16)
    vt = _dot_nt(wuvt_ref[...], ckv).astype(BF16)
    ones = jnp.ones((VT_ROWS - V_HEAD_DIM, vt.shape[1]), BF16)
    for hh in range(MLA_HEADS):
        vt_ref[0, hh, 0, 0:V_HEAD_DIM, :] = vt[hh * V_HEAD_DIM:(hh + 1) * V_HEAD_DIM, :]
        vt_ref[0, hh, 0, V_HEAD_DIM:VT_ROWS, :] = ones


def _mla_operands(d, g, w_down, q_norm, w_uq, kv_norm, w_ukv, tabs):
    nh, dn_, dr, dv = MLA_HEADS, QK_NOPE_DIM, QK_ROPE_DIM, V_HEAD_DIM
    kr_lo = Q_LORA_RANK + KV_LORA_RANK
    w_dn = jnp.concatenate([
        w_down[:, :kr_lo], jnp.zeros((d, dn_), F32), w_down[:, kr_lo:],
        jnp.zeros((d, HEAD_PAD - dn_ - dr), F32)], axis=1).astype(BF16)
    wq = w_uq.reshape(Q_LORA_RANK, nh, dn_ + dr)
    wq = jnp.pad(wq, ((0, 0), (0, 0), (0, HEAD_PAD - dn_ - dr))).reshape(Q_LORA_RANK, nh * HEAD_PAD).astype(BF16)
    wkv = w_ukv.reshape(KV_LORA_RANK, nh, dn_ + dv)
    wk = jnp.pad(wkv[:, :, :dn_], ((0, 0), (0, 0), (0, HEAD_PAD - dn_))).reshape(KV_LORA_RANK, nh * HEAD_PAD).astype(BF16)
    wvt = wkv[:, :, dn_:].reshape(KV_LORA_RANK, nh * dv).T.astype(BF16)
    return (g.reshape(1, d), w_dn, q_norm.reshape(1, -1), wq, kv_norm.reshape(1, -1), wk, wvt) + tuple(tabs)


FLASH_HEADS = 4


def _flash_kernel(q_ref, k_ref, vt_ref, o_ref, sa_ref, sb_ref, acc_ref):
    tq = q_ref.shape[1]
    tk = vt_ref.shape[4]
    qi = pl.program_id(2)

    def scores(j, s_ref, c0=0):
        k2 = k_ref[0, pl.ds(pl.multiple_of(j * tk, tk), tk), :]
        for hh in range(FLASH_HEADS):
            sl = slice(hh * HEAD_PAD, (hh + 1) * HEAD_PAD)
            s_ref[hh, :, c0:] = _dot_nt(k2[:, sl], q_ref[0, c0:, sl])

    def consume(j, s_ref, ms, stream, c0=0, diag=False):
        out = []
        for hh in range(FLASH_HEADS):
            st = s_ref[hh, :, c0:]
            if diag:
                key = lax.broadcasted_iota(I32, st.shape, 0)
                qry = lax.broadcasted_iota(I32, st.shape, 1)
                st = jnp.where(key <= qry, st, NEG_BIG)
            m = ms[hh][:, c0:]
            m_new = jnp.maximum(m, jnp.max(st, axis=0, keepdims=True))
            alpha = jnp.exp2(m - m_new)
            p = jnp.exp2(st - m_new).astype(BF16)
            acc_ref[stream, hh, :, c0:] = alpha * acc_ref[stream, hh, :, c0:] + _dot(vt_ref[0, hh, j], p)
            out.append(m_new if c0 == 0 else jnp.concatenate([ms[hh][:, :c0], m_new], axis=1))
        return tuple(out)

    acc_ref[...] = jnp.zeros_like(acc_ref)
    ms0 = tuple(jnp.full((1, tq), NEG_BIG, F32) for _ in range(2 * FLASH_HEADS))
    scores(0, sa_ref)

    def two_tiles(i, ms):
        scores(2 * i + 1, sb_ref)
        even = consume(2 * i, sa_ref, ms[:FLASH_HEADS], 0)
        scores(2 * i + 2, sa_ref)
        return even + consume(2 * i + 1, sb_ref, ms[FLASH_HEADS:], 1)

    assert tq == 2 * tk
    ms = lax.fori_loop(0, qi, two_tiles, ms0)
    scores(2 * qi + 1, sb_ref, c0=tk)
    even = consume(2 * qi, sa_ref, ms[:FLASH_HEADS], 0, diag=True)
    odd = consume(2 * qi + 1, sb_ref, ms[FLASH_HEADS:], 1, c0=tk, diag=True)
    heads = []
    for hh in range(FLASH_HEADS):
        m = jnp.maximum(even[hh], odd[hh])
        acc = jnp.exp2(even[hh] - m) * acc_ref[0, hh] + jnp.exp2(odd[hh] - m) * acc_ref[1, hh]
        heads.append(acc[0:V_HEAD_DIM, :] * (1.0 / acc[V_HEAD_DIM:V_HEAD_DIM + 1, :]))
    o_ref[0] = jnp.concatenate(heads, axis=0).T.astype(BF16)


def _flash(q, k, vt, tq=512):
    b, s, _ = q.shape
    _, nh, nblk, rows, tk = vt.shape
    return pl.pallas_call(
        _flash_kernel,
        grid=(b, nh // FLASH_HEADS, s // tq),
        in_specs=[
            pl.BlockSpec((1, tq, FLASH_HEADS * HEAD_PAD), lambda i, p, j: (i, j, p)),
            pl.BlockSpec((1, s, FLASH_HEADS * HEAD_PAD), lambda i, p, j: (i, 0, p)),
            pl.BlockSpec((1, FLASH_HEADS, nblk, rows, tk), lambda i, p, j: (i, p, 0, 0, 0)),
        ],
        out_specs=pl.BlockSpec((1, tq, FLASH_HEADS * V_HEAD_DIM), lambda i, p, j: (i, j, p)),
        out_shape=jax.ShapeDtypeStruct((b, s, nh * V_HEAD_DIM), BF16),
        scratch_shapes=[pltpu.VMEM((FLASH_HEADS, tk, tq), F32), pltpu.VMEM((FLASH_HEADS, tk, tq), F32),
                        pltpu.VMEM((2, FLASH_HEADS, rows, tq), F32)],
        compiler_params=_params("arbitrary", "arbitrary", "arbitrary"),
        name="flash",
    )(q, k, vt)


def kernel(x, mem, positions, norm_mix, conv_w_in, conv_b_in, conv_w_dw, conv_b_dw, conv_ln_g, conv_ln_b, conv_w_out, conv_b_out, mla_w_down, mla_q_norm, mla_w_uq, mla_kv_norm, mla_w_ukv, mla_w_o, norm_xa, norm_mem, xa_wq, xa_wkv, xa_wo, norm_ffn, moe_w_grp, moe_b_grp, moe_w_exp, moe_b_exp, moe_w_gate, moe_w_up, moe_w_down, final_norm):
    tabs = _rope_tables(positions)
    kv_mem = _mem_kv(mem, norm_mem, xa_wkv.astype(BF16))

    def ffn(h, i, final, mla=None):
        return _moe(h, norm_ffn[i], moe_w_grp[i], moe_b_grp[i], moe_w_exp[i], moe_b_exp[i],
                    moe_w_gate, moe_w_up, moe_w_down, i, final_norm, final, mla)

    h = _conv_mixer(x, norm_mix[0], conv_w_in[0], conv_b_in[0], conv_w_dw[0], conv_b_dw[0],
                    conv_ln_g[0], conv_ln_b[0], conv_w_out[0], conv_b_out[0])
    h = _xattn(h, norm_xa[0], xa_wq[0], kv_mem, 0, xa_wo[0])
    mla = _mla_operands(x.shape[-1], norm_mix[1], mla_w_down[0], mla_q_norm[0], mla_w_uq[0], mla_kv_norm[0],
                        mla_w_ukv[0], tabs)
    h, q, k, vt = ffn(h, 0, False, mla)
    attn = _flash(q, k, vt)
    h = _xattn(h, norm_xa[1], xa_wq[1], kv_mem, 1, xa_wo[1], prefix=(attn, mla_w_o[0]))
    return ffn(h, 1, True)
```

```python
import functools
import math

import jax
import jax.numpy as jnp
from jax import lax
from jax.experimental import pallas as pl
from jax.experimental.pallas import tpu as pltpu

F32 = jnp.float32
BF16 = jnp.bfloat16
I32 = jnp.int32

NORM_EPS = 1e-6
CONV_WIDTH = 31
CONV_HALO = 32
SUBLANES = 8
MLA_HEADS = 16
QK_NOPE_DIM = 64
QK_ROPE_DIM = 32
V_HEAD_DIM = 64
Q_LORA_RANK = 384
KV_LORA_RANK = 256
ROPE_THETA = 10000.0
HEAD_PAD = 128
XA_HEADS = 4
N_GROUPS = 4
EXPERTS_PER_GROUP = 8
N_EXPERTS = N_GROUPS * EXPERTS_PER_GROUP
TOP_K = 2
ROUTER_LANES = 128
NEG_BIG = -1e30
LOG2_E = 1.4426950408889634

VMEM_LIMIT = 56 * 1024 * 1024


def _rms(x, g):
    return x * lax.rsqrt(jnp.mean(x * x, axis=-1, keepdims=True) + NORM_EPS) * g


def _dot(a, b):
    return jnp.dot(a, b, preferred_element_type=F32)


def _dot_nt(a, b):
    return lax.dot_general(a, b, (((1,), (1,)), ((), ())), preferred_element_type=F32)


def _params(*sem):
    return pltpu.CompilerParams(dimension_semantics=sem, vmem_limit_bytes=VMEM_LIMIT)


def _const_spec(shape):
    return pl.BlockSpec(shape, lambda *_: (0,) * len(shape))


def _rope_kernel(pos_ref, freq_ref, cos_ref, sin_ref):
    ang = pos_ref[...].astype(F32) * freq_ref[...]
    cos_ref[...] = jnp.cos(ang)
    sin_ref[...] = jnp.sin(ang)


def _rope_tables(positions):
    half = QK_ROPE_DIM // 2
    t = positions.size
    inv_freq = 1.0 / (ROPE_THETA ** (jnp.arange(0, QK_ROPE_DIM, 2, dtype=F32) / QK_ROPE_DIM))
    rows = t * half // 128
    pos_rep = jnp.repeat(positions.reshape(-1), half).reshape(rows, 128)
    freq = jnp.tile(inv_freq, 128 // half).reshape(1, 128)
    cos, sin = pl.pallas_call(
        _rope_kernel,
        out_shape=(jax.ShapeDtypeStruct((rows, 128), F32),) * 2,
        name="rope_tables",
    )(pos_rep, freq)
    cos = cos.reshape(t, half)
    sin = sin.reshape(t, half)
    one = jnp.ones((t, QK_NOPE_DIM), F32)
    z16 = jnp.zeros((t, half), F32)
    z64 = jnp.zeros((t, QK_NOPE_DIM), F32)
    pad1 = jnp.ones((t, HEAD_PAD - QK_NOPE_DIM - QK_ROPE_DIM), F32)
    pad0 = jnp.zeros((t, HEAD_PAD - QK_NOPE_DIM - QK_ROPE_DIM), F32)
    tab_c = jnp.concatenate([one, cos, cos, pad1], axis=1)
    tab_a = jnp.concatenate([z64, sin, z16, pad0], axis=1)
    tab_b = jnp.concatenate([z64, z16, sin, pad0], axis=1)
    return tab_c, tab_a, tab_b


def _rope_apply(x, c, a, b):
    return x * c - pltpu.roll(x, HEAD_PAD - 16, 1) * a + pltpu.roll(x, 16, 1) * b


def _mem_kv_kernel(mem_ref, g_ref, w_ref, o_ref):
    mn = _rms(mem_ref[0], g_ref[0]).astype(BF16)
    o_ref[0, 0] = _dot(mn, w_ref[0]).astype(BF16)


def _mem_kv(mem, norm_mem, wkv):
    b, m, d = mem.shape
    l = norm_mem.shape[0]
    return pl.pallas_call(
        _mem_kv_kernel,
        grid=(l, b),
        in_specs=[
            pl.BlockSpec((1, m, d), lambda i, j: (j, 0, 0)),
            pl.BlockSpec((1, 1, d), lambda i, j: (i, 0, 0)),
            pl.BlockSpec((1, d, 2 * d), lambda i, j: (i, 0, 0)),
        ],
        out_specs=pl.BlockSpec((1, 1, m, 2 * d), lambda i, j: (i, j, 0, 0)),
        out_shape=jax.ShapeDtypeStruct((l, b, m, 2 * d), BF16),
        compiler_params=_params("arbitrary", "arbitrary"),
        name="mem_kv",
    )(mem, norm_mem.reshape(l, 1, d), wkv)


CONV_ROWS = 16
CONV_COLS = 4


def _conv_mixer_kernel(h_ref, g_ref, win_ref, bin_ref, wdw_ref, bdw_ref, lng_ref, lnb_ref,
                       wout_ref, bout_ref, o_ref, cb_ref, cv_ref):
    ts, d = h_ref.shape[1], h_ref.shape[2]

    @pl.when(pl.program_id(1) == 0)
    def _():
        cb_ref[0, 0:CONV_HALO, :] = jnp.zeros((CONV_HALO, d), F32)

    h = h_ref[0]
    hn = _rms(h, g_ref[...]).astype(BF16)
    u = _dot(hn, win_ref[...]) + bin_ref[...]
    cb_ref[0, CONV_HALO:CONV_HALO + ts, :] = u[:, :d] * jax.nn.sigmoid(u[:, d:])
    ncopy = ts + CONV_HALO - SUBLANES
    for s in range(1, SUBLANES):
        cb_ref[s, 0:ncopy, :] = cb_ref[0, s:s + ncopy, :]

    base = CONV_HALO - (CONV_WIDTH - 1)

    def rows(i, carry):
        r0 = pl.multiple_of(i * CONV_ROWS, CONV_ROWS)
        for cg in range(0, d, LANES * CONV_COLS):
            cols = [cg + LANES * c for c in range(CONV_COLS)]
            accs = [jnp.zeros((CONV_ROWS, LANES), F32) for _ in cols]
            for k in range(CONV_WIDTH):
                off = base + k
                row0 = pl.multiple_of(r0 + (off // SUBLANES) * SUBLANES, SUBLANES)
                for ci, c0 in enumerate(cols):
                    tap = cb_ref[off % SUBLANES, pl.ds(row0, CONV_ROWS), c0:c0 + LANES]
                    accs[ci] = accs[ci] + tap * wdw_ref[k:k + 1, c0:c0 + LANES]
            for ci, c0 in enumerate(cols):
                cv_ref[pl.ds(r0, CONV_ROWS), c0:c0 + LANES] = accs[ci]
        return carry

    lax.fori_loop(0, ts // CONV_ROWS, rows, 0)
    cb_ref[0, 0:CONV_HALO, :] = cb_ref[0, ts:ts + CONV_HALO, :]

    v = cv_ref[...] + bdw_ref[...]
    mu = jnp.mean(v, axis=-1, keepdims=True)
    vc = v - mu
    var = jnp.mean(vc * vc, axis=-1, keepdims=True)
    y = vc * lax.rsqrt(var + NORM_EPS) * lng_ref[...] + lnb_ref[...]
    y = (y * jax.nn.sigmoid(y)).astype(BF16)
    o_ref[0] = h + _dot(y, wout_ref[...]) + bout_ref[...]


def _conv_mixer(h, g, w_in, b_in, w_dw, b_dw, ln_g, ln_b, w_out, b_out, ts=256):
    b, s, d = h.shape
    row = lambda v: v.reshape(1, -1)
    return pl.pallas_call(
        _conv_mixer_kernel,
        grid=(b, s // ts),
        in_specs=[
            pl.BlockSpec((1, ts, d), lambda i, j: (i, j, 0)),
            _const_spec((1, d)), _const_spec((d, 2 * d)), _const_spec((1, 2 * d)),
            _const_spec((CONV_WIDTH, d)), _const_spec((1, d)), _const_spec((1, d)), _const_spec((1, d)),
            _const_spec((d, d)), _const_spec((1, d)),
        ],
        out_specs=pl.BlockSpec((1, ts, d), lambda i, j: (i, j, 0)),
        out_shape=jax.ShapeDtypeStruct((b, s, d), F32),
        scratch_shapes=[pltpu.VMEM((SUBLANES, ts + CONV_HALO, d), F32), pltpu.VMEM((ts, d), F32)],
        compiler_params=_params("arbitrary", "arbitrary"),
        name="conv_mixer",
    )(h, row(g), w_in.astype(BF16), row(b_in), w_dw, row(b_dw), row(ln_g), row(ln_b),
      w_out.astype(BF16), row(b_out))


def _xattn_kernel(*refs, with_prefix):
    if with_prefix:
        h_ref, a_ref, wmo_ref, g_ref, wq_ref, kv_ref, wo_ref, o_ref = refs
        h = h_ref[0] + _dot(a_ref[0], wmo_ref[...])
    else:
        h_ref, g_ref, wq_ref, kv_ref, wo_ref, o_ref = refs
        h = h_ref[0]
    d = h.shape[-1]
    hd = d // XA_HEADS
    hn = _rms(h, g_ref[...]).astype(BF16)
    q = (_dot(hn, wq_ref[...]) * (1.0 / math.sqrt(hd))).astype(BF16)
    outs = []
    for hh in range(XA_HEADS):
        k = kv_ref[0, 0, :, hh * hd:(hh + 1) * hd]
        v = kv_ref[0, 0, :, d + hh * hd:d + (hh + 1) * hd]
        sc = _dot_nt(q[:, hh * hd:(hh + 1) * hd], k)
        p = jnp.exp(sc - jnp.max(sc, axis=-1, keepdims=True))
        inv = 1.0 / jnp.sum(p, axis=-1, keepdims=True)
        outs.append((_dot(p.astype(BF16), v) * inv).astype(BF16))
    o = jnp.concatenate(outs, axis=-1)
    o_ref[0] = h + _dot(o, wo_ref[...])


def _xattn(h, g, wq, kv, layer, wo, prefix=None, ts=512):
    b, s, d = h.shape
    m = kv.shape[2]
    tile = pl.BlockSpec((1, ts, d), lambda i, j: (i, j, 0))
    in_specs = [tile]
    args = [h]
    if prefix is not None:
        attn, w_mo = prefix
        in_specs += [tile, _const_spec((d, d))]
        args += [attn, w_mo.astype(BF16)]
    in_specs += [_const_spec((1, d)), _const_spec((d, d)),
                 pl.BlockSpec((1, 1, m, 2 * d), lambda i, j: (layer, i, 0, 0)), _const_spec((d, d))]
    args += [g.reshape(1, d), wq.astype(BF16), kv, wo.astype(BF16)]
    return pl.pallas_call(
        functools.partial(_xattn_kernel, with_prefix=prefix is not None),
        grid=(b, s // ts),
        in_specs=in_specs,
        out_specs=tile,
        out_shape=jax.ShapeDtypeStruct((b, s, d), F32),
        compiler_params=_params("arbitrary", "arbitrary"),
        name="xattn",
    )(*args)


ROUTE_TILE = 512
EXPERT_TILE = 512
COMBINE_TILE = 512
DISPATCH_TILE = 512
LANES = 128


def _load_slabs(ref, rows, d):
    n = d // LANES
    return jnp.concatenate([ref[pl.ds(j, rows, stride=n), :] for j in range(n)], axis=1)


def _store_slabs(ref, val, accumulate=False):
    rows, d = val.shape
    n = d // LANES
    for j in range(n):
        chunk = val[:, j * LANES:(j + 1) * LANES]
        if accumulate:
            chunk = chunk + ref[pl.ds(j, rows, stride=n), :]
        ref[pl.ds(j, rows, stride=n), :] = chunk


INFO_E1, INFO_E2, INFO_R1, INFO_R2, INFO_W1, INFO_W2 = range(6)
ROUTER_EXPERT_ROW = SUBLANES
ROUTER_ROWS = ROUTER_EXPERT_ROW + N_EXPERTS + SUBLANES


def _router_kernel(h_ref, g_ref, wrt_ref, brt_ref, hn_ref, info_ref, cnt_ref, carry_ref):
    tm = h_ref.shape[0]
    assert EXPERTS_PER_GROUP == SUBLANES

    @pl.when(pl.program_id(0) == 0)
    def _():
        carry_ref[...] = jnp.zeros_like(carry_ref)

    hn = _rms(h_ref[...], g_ref[...])
    _store_slabs(hn_ref, hn)
    logits = lax.dot_general(wrt_ref[...], hn, (((1,), (1,)), ((), ())), preferred_element_type=F32,
                             precision=lax.Precision.HIGHEST) + brt_ref[:, 0:1]
    sub = lax.broadcasted_iota(I32, (SUBLANES, tm), 0)
    gl = jnp.where(sub < N_GROUPS, logits[0:SUBLANES], NEG_BIG)
    gmax = jnp.max(gl, axis=0, keepdims=True)
    gsum = jnp.sum(jnp.exp(gl - gmax), axis=0, keepdims=True)
    g_val = 1.0 / gsum
    g_idx = jnp.min(jnp.where(gl == gmax, sub, SUBLANES), axis=0, keepdims=True)
    el = logits[ROUTER_EXPERT_ROW:ROUTER_EXPERT_ROW + SUBLANES]
    for grp in range(1, N_GROUPS):
        lo = ROUTER_EXPERT_ROW + grp * SUBLANES
        el = jnp.where(g_idx == grp, logits[lo:lo + SUBLANES], el)
    m1 = jnp.max(el, axis=0, keepdims=True)
    esum = jnp.sum(jnp.exp(el - m1), axis=0, keepdims=True)
    i1 = jnp.min(jnp.where(el == m1, sub, SUBLANES), axis=0, keepdims=True)
    el2 = jnp.where(sub == i1, NEG_BIG, el)
    m2 = jnp.max(el2, axis=0, keepdims=True)
    i2 = jnp.min(jnp.where(el2 == m2, sub, SUBLANES), axis=0, keepdims=True)
    p1 = 1.0 / esum
    p2 = jnp.exp(m2 - m1) / esum
    w1 = g_val * (p1 / (p1 + p2))
    w2 = g_val * (p2 / (p1 + p2))
    e1 = g_idx * EXPERTS_PER_GROUP + i1
    e2 = g_idx * EXPERTS_PER_GROUP + i2
    erow = lax.broadcasted_iota(I32, (N_EXPERTS, tm), 0)
    hit = jnp.where((erow == e1) | (erow == e2), 1.0, 0.0)
    r = lax.broadcasted_iota(I32, (tm, tm), 0)
    c = lax.broadcasted_iota(I32, (tm, tm), 1)
    before = jnp.where(r < c, 1.0, 0.0).astype(BF16)
    tot = _dot(hit.astype(BF16), before) + carry_ref[:, 0:1]
    rank1 = jnp.sum(jnp.where(erow == e1, tot, 0.0), axis=0, keepdims=True)
    rank2 = jnp.sum(jnp.where(erow == e2, tot, 0.0), axis=0, keepdims=True)
    carry_ref[...] = carry_ref[...] + jnp.sum(hit, axis=1, keepdims=True)
    cnt_ref[...] = carry_ref[...]
    info = jnp.zeros((SUBLANES, tm), F32)
    for row, val in ((INFO_E1, e1.astype(F32)), (INFO_E2, e2.astype(F32)), (INFO_R1, rank1),
                     (INFO_R2, rank2), (INFO_W1, w1), (INFO_W2, w2)):
        info = jnp.where(sub == row, val, info)
    info_ref[...] = info


DISPATCH_SLOTS = 3


def _dispatch_kernel(p1_ref, p2_ref, hn_ref, xs_ref, buf, lsem, ssem):
    i = pl.program_id(0)
    n = pl.num_programs(0)
    slab = xs_ref.shape[1]
    rows = buf.shape[1]
    tm = rows // slab
    slot = i % DISPATCH_SLOTS

    def load(step):
        sl = step % DISPATCH_SLOTS
        return pltpu.make_async_copy(hn_ref.at[pl.ds(step * rows, rows)], buf.at[sl], lsem.at[sl])

    def wait_scatters(sl):
        cp = pltpu.make_async_copy(buf.at[sl], hn_ref.at[pl.ds(0, rows)], ssem.at[sl])
        for _ in range(TOP_K):
            cp.wait()

    @pl.when(i == 0)
    def _():
        load(0).start()

    @pl.when(i + 1 < n)
    def _():
        load(i + 1).start()

    load(i).wait()

    def tok(r, carry):
        tt = i * tm + r
        src = buf.at[slot, pl.ds(pl.multiple_of(r * slab, slab), slab)]
        pltpu.make_async_copy(src, xs_ref.at[p1_ref[tt]], ssem.at[slot]).start()
        pltpu.make_async_copy(src, xs_ref.at[p2_ref[tt]], ssem.at[slot]).start()
        return carry

    lax.fori_loop(0, tm, tok, 0, unroll=8)

    @pl.when(i > 0)
    def _():
        wait_scatters((i - 1) % DISPATCH_SLOTS)

    @pl.when(i == n - 1)
    def _():
        wait_scatters(slot)


def _expert_kernel(ve_ref, vt_ref, lo_ref, hi_ref, nv_ref, x_ref, wg_ref, wu_ref, wd_ref, y_ref,
                   wgu_s, wd_s, gu_s):
    v = pl.program_id(0)
    d, f = wg_ref.shape[2], wg_ref.shape[3]
    tm = x_ref.shape[0] * LANES // d

    @pl.when(v < nv_ref[0])
    def _():
        pv = jnp.maximum(v - 1, 0)

        @pl.when((v == 0) | (ve_ref[v] != ve_ref[pv]))
        def _():
            wgu_s[:, :f] = wg_ref[0, 0].astype(BF16)
            wgu_s[:, f:] = wu_ref[0, 0].astype(BF16)
            wd_s[...] = wd_ref[0, 0].astype(BF16)

        half = tm // 2
        slab = d // LANES

        @pl.when((v == 0) | (vt_ref[v] != vt_ref[pv]))
        def _():
            y_ref[...] = jnp.zeros_like(y_ref)

        for hh in range(2):
            x = _load_slabs(x_ref.at[pl.ds(hh * half * slab, half * slab)], half, d)
            gu_s[hh] = _dot(x.astype(BF16), wgu_s[...])
        for hh in range(2):
            gate = gu_s[hh, :, :f]
            he = (gate * jax.nn.sigmoid(gate) * gu_s[hh, :, f:]).astype(BF16)
            row = lax.broadcasted_iota(I32, (half, 1), 0) + hh * half
            y = jnp.where((row >= lo_ref[v]) & (row < hi_ref[v]), _dot(he, wd_s[...]), 0.0)
            _store_slabs(y_ref.at[pl.ds(hh * half * slab, half * slab)], y, accumulate=True)


def _combine_kernel(p1_ref, p2_ref, h_ref, info_ref, fin_ref, ys_ref, ys2d_ref, *rest, final, with_mla):
    if with_mla:
        mla_in, (o_ref, q_ref, k_ref, vt_ref, buf, sem) = rest[:10], rest[10:]
    else:
        o_ref, buf, sem = rest
    i = pl.program_id(0)
    n = pl.num_programs(0)
    tm, d = h_ref.shape
    slab = ys_ref.shape[1]
    slot = i % 2

    def issue(step, sl):
        def tok(r, carry):
            tt = step * tm + r
            dst = pl.ds(pl.multiple_of(r * slab, slab), slab)
            pltpu.make_async_copy(ys_ref.at[p1_ref[tt]], buf.at[sl, 0, dst], sem.at[sl]).start()
            pltpu.make_async_copy(ys_ref.at[p2_ref[tt]], buf.at[sl, 1, dst], sem.at[sl]).start()
            return carry

        lax.fori_loop(0, tm, tok, 0, unroll=8)

    @pl.when(i == 0)
    def _():
        issue(0, 0)

    @pl.when(i + 1 < n)
    def _():
        issue(i + 1, 1 - slot)

    for k in range(TOP_K):
        pltpu.make_async_copy(ys2d_ref.at[pl.ds(0, tm * slab)], buf.at[slot, k], sem.at[slot]).wait()
    info = jnp.concatenate([info_ref[...], jnp.zeros((ROUTER_LANES - SUBLANES, tm), F32)], axis=0).T
    out = (h_ref[...] + info[:, INFO_W1:INFO_W1 + 1] * _load_slabs(buf.at[slot, 0], tm, d)
           + info[:, INFO_W2:INFO_W2 + 1] * _load_slabs(buf.at[slot, 1], tm, d))
    if final:
        out = _rms(out, fin_ref[...])
    o_ref[...] = out
    if with_mla:
        _mla_tile(out, *mla_in, q_ref, k_ref, vt_ref)


def _moe(h, g, w_grp, b_grp, w_exp, b_exp, w_gate, w_up, w_down, layer, final_norm, final, mla=None):
    b, s, d = h.shape
    t = b * s
    f = w_gate.shape[-1]
    slab = d // LANES
    assert slab == SUBLANES
    h2 = h.reshape(t, d)
    zrow = lambda n: jnp.zeros((n, d), F32)
    wrt = jnp.concatenate([w_grp.T, zrow(ROUTER_EXPERT_ROW - N_GROUPS), w_exp.T,
                           zrow(ROUTER_ROWS - ROUTER_EXPERT_ROW - N_EXPERTS)], axis=0)
    brt = jnp.concatenate([b_grp, jnp.zeros((ROUTER_EXPERT_ROW - N_GROUPS,), F32), b_exp,
                           jnp.zeros((ROUTER_ROWS - ROUTER_EXPERT_ROW - N_EXPERTS,), F32)])
    brt = jnp.broadcast_to(brt[:, None], (ROUTER_ROWS, ROUTER_LANES))

    hn, info, cnt = pl.pallas_call(
        _router_kernel,
        grid=(t // ROUTE_TILE,),
        in_specs=[pl.BlockSpec((ROUTE_TILE, d), lambda i: (i, 0)), _const_spec((1, d)),
                  _const_spec((ROUTER_ROWS, d)), _const_spec((ROUTER_ROWS, ROUTER_LANES))],
        out_specs=(pl.BlockSpec((ROUTE_TILE * slab, LANES), lambda i: (i, 0)),
                   pl.BlockSpec((SUBLANES, ROUTE_TILE), lambda i: (0, i)),
                   _const_spec((N_EXPERTS, ROUTER_LANES))),
        out_shape=(jax.ShapeDtypeStruct((t * slab, LANES), F32), jax.ShapeDtypeStruct((SUBLANES, t), F32),
                   jax.ShapeDtypeStruct((N_EXPERTS, ROUTER_LANES), F32)),
        scratch_shapes=[pltpu.VMEM((N_EXPERTS, ROUTER_LANES), F32)],
        compiler_params=_params("arbitrary"),
        name="moe_router",
    )(h2, g.reshape(1, d), wrt, brt)

    tm = EXPERT_TILE
    n_rows = TOP_K * t
    n_visits = n_rows // tm + N_EXPERTS - 1
    counts = cnt[:, 0].astype(I32)
    row_end = jnp.cumsum(counts)
    row_start = row_end - counts
    first_tile = row_start // tm
    e_visits = jnp.where(counts > 0, (row_end - 1) // tm - first_tile + 1, 0)
    v_end = jnp.cumsum(e_visits)
    v_start = v_end - e_visits
    n_v = v_end[-1:]
    vis = jnp.minimum(jnp.arange(n_visits, dtype=I32), n_v - 1)
    v_e = jnp.sum(vis[:, None] >= v_end[None, :], axis=1).astype(I32)
    v_tile = (first_tile[v_e] + vis - v_start[v_e]).astype(I32)
    v_lo = (jnp.maximum(row_start[v_e], v_tile * tm) - v_tile * tm).astype(I32)
    v_hi = (jnp.minimum(row_end[v_e], (v_tile + 1) * tm) - v_tile * tm).astype(I32)
    eid = jnp.arange(N_EXPERTS, dtype=I32)[:, None]
    start_of = lambda e: jnp.sum(jnp.where(e.astype(I32)[None, :] == eid, row_start[:, None], 0), axis=0)
    pos1 = start_of(info[INFO_E1]) + info[INFO_R1].astype(I32)
    pos2 = start_of(info[INFO_E2]) + info[INFO_R2].astype(I32)

    xs = pl.pallas_call(
        _dispatch_kernel,
        grid_spec=pltpu.PrefetchScalarGridSpec(
            num_scalar_prefetch=2, grid=(t // DISPATCH_TILE,),
            in_specs=[pl.BlockSpec(memory_space=pl.ANY)],
            out_specs=pl.BlockSpec(memory_space=pl.ANY),
            scratch_shapes=[pltpu.VMEM((DISPATCH_SLOTS, DISPATCH_TILE * slab, LANES), F32),
                            pltpu.SemaphoreType.DMA((DISPATCH_SLOTS,)),
                            pltpu.SemaphoreType.DMA((DISPATCH_SLOTS,))],
        ),
        out_shape=jax.ShapeDtypeStruct((n_rows, slab, LANES), F32),
        compiler_params=_params("arbitrary"),
        name="moe_dispatch",
    )(pos1, pos2, hn)

    by_tile = lambda v, ve, vt, lo, hi, nv: (vt[v], 0)
    by_expert = lambda v, ve, vt, lo, hi, nv: (layer, ve[v], 0, 0)
    ys = pl.pallas_call(
        _expert_kernel,
        grid_spec=pltpu.PrefetchScalarGridSpec(
            num_scalar_prefetch=5, grid=(n_visits,),
            in_specs=[
                pl.BlockSpec((tm * slab, LANES), by_tile),
                pl.BlockSpec((1, 1, d, f), by_expert),
                pl.BlockSpec((1, 1, d, f), by_expert),
                pl.BlockSpec((1, 1, f, d), by_expert),
            ],
            out_specs=pl.BlockSpec((tm * slab, LANES), by_tile),
            scratch_shapes=[pltpu.VMEM((d, 2 * f), BF16), pltpu.VMEM((f, d), BF16),
                            pltpu.VMEM((2, tm // 2, 2 * f), F32)],
        ),
        out_shape=jax.ShapeDtypeStruct((n_rows * slab, LANES), F32),
        compiler_params=_params("arbitrary"),
        name="moe_experts",
    )(v_e, v_tile, v_lo, v_hi, n_v, xs.reshape(n_rows * slab, LANES), w_gate, w_up, w_down)

    tc = COMBINE_TILE
    tok = lambda w: pl.BlockSpec((tc, w), lambda i, p1, p2: (i, 0))
    in_specs = [tok(d), pl.BlockSpec((SUBLANES, tc), lambda i, p1, p2: (0, i)),
                pl.BlockSpec((1, d), lambda i, p1, p2: (0, 0)),
                pl.BlockSpec(memory_space=pl.ANY), pl.BlockSpec(memory_space=pl.ANY)]
    args = [h2, info, final_norm.reshape(1, d), ys.reshape(n_rows, slab, LANES), ys]
    out_specs = tok(d)
    out_shape = jax.ShapeDtypeStruct((t, d), F32)
    if mla is not None:
        assert tc % MLA_TILE == 0 and s % tc == 0
        nblk = s // tc
        nsub = tc // MLA_TILE
        width = MLA_HEADS * HEAD_PAD
        in_specs += [pl.BlockSpec(w.shape, lambda i, p1, p2, nd=w.ndim: (0,) * nd) for w in mla[:7]]
        in_specs += [tok(HEAD_PAD)] * 3
        args += list(mla)
        out_specs = (out_specs, tok(width), tok(width),
                     pl.BlockSpec((1, MLA_HEADS, nsub, VT_ROWS, MLA_TILE),
                                  lambda i, p1, p2: (i // nblk, 0, i % nblk, 0, 0)))
        out_shape = (out_shape, jax.ShapeDtypeStruct((t, width), BF16), jax.ShapeDtypeStruct((t, width), BF16),
                     jax.ShapeDtypeStruct((b, MLA_HEADS, s // MLA_TILE, VT_ROWS, MLA_TILE), BF16))
    res = pl.pallas_call(
        functools.partial(_combine_kernel, final=final, with_mla=mla is not None),
        grid_spec=pltpu.PrefetchScalarGridSpec(
            num_scalar_prefetch=2, grid=(t // tc,),
            in_specs=in_specs,
            out_specs=out_specs,
            scratch_shapes=[pltpu.VMEM((2, TOP_K, tc * slab, LANES), F32), pltpu.SemaphoreType.DMA((2,))],
        ),
        out_shape=out_shape,
        compiler_params=_params("arbitrary"),
        name="moe_combine",
    )(pos1, pos2, *args)
    if mla is None:
        return res.reshape(b, s, d)
    out, q, k, vt = res
    return out.reshape(b, s, d), q.reshape(b, s, -1), k.reshape(b, s, -1), vt


MLA_TILE = 256
VT_ROWS = V_HEAD_DIM + 16


def _mla_tile(h, g_ref, wdn_ref, qn_ref, wuq_ref, kvn_ref, wuk_ref, wuvt_ref, tc_ref, ta_ref, tb_ref,
              q_ref, k_ref, vt_ref):
    hn = _rms(h, g_ref[...]).astype(BF16)
    dn = _dot(hn, wdn_ref[...])
    c, a, b = tc_ref[...], ta_ref[...], tb_ref[...]
    scale = LOG2_E / math.sqrt(QK_NOPE_DIM + QK_ROPE_DIM)
    kv_lo = Q_LORA_RANK
    kr_lo = Q_LORA_RANK + KV_LORA_RANK
    cq = _rms(dn[:, :kv_lo], qn_ref[...]).astype(BF16)
    q = _dot(cq, wuq_ref[...])
    for hh in range(MLA_HEADS):
        sl = slice(hh * HEAD_PAD, (hh + 1) * HEAD_PAD)
        q_ref[:, sl] = (_rope_apply(q[:, sl], c, a, b) * scale).astype(BF16)
    ckv = _rms(dn[:, kv_lo:kr_lo], kvn_ref[...]).astype(BF16)
    kn = _dot(ckv, wuk_ref[...])
    kr = _rope_apply(dn[:, kr_lo:kr_lo + HEAD_PAD], c, a, b)
    for hh in range(MLA_HEADS):
        sl = slice(hh * HEAD_PAD, (hh + 1) * HEAD_PAD)
        k_ref[:, sl] = (kn[:, sl] + kr).astype(BF16)
    vt = _dot_nt(wuvt_ref[...], ckv).astype(BF16)
    tk = vt_ref.shape[4]
    ones = jnp.ones((VT_ROWS - V_HEAD_DIM, tk), BF16)
    for hh in range(MLA_HEADS):
        for sb in range(vt_ref.shape[2]):
            vt_ref[0, hh, sb, 0:V_HEAD_DIM, :] = vt[hh * V_HEAD_DIM:(hh + 1) * V_HEAD_DIM, sb * tk:(sb + 1) * tk]
            vt_ref[0, hh, sb, V_HEAD_DIM:VT_ROWS, :] = ones


def _mla_operands(d, g, w_down, q_norm, w_uq, kv_norm, w_ukv, tabs):
    nh, dn_, dr, dv = MLA_HEADS, QK_NOPE_DIM, QK_ROPE_DIM, V_HEAD_DIM
    kr_lo = Q_LORA_RANK + KV_LORA_RANK
    w_dn = jnp.concatenate([
        w_down[:, :kr_lo], jnp.zeros((d, dn_), F32), w_down[:, kr_lo:],
        jnp.zeros((d, HEAD_PAD - dn_ - dr), F32)], axis=1).astype(BF16)
    wq = w_uq.reshape(Q_LORA_RANK, nh, dn_ + dr)
    wq = jnp.pad(wq, ((0, 0), (0, 0), (0, HEAD_PAD - dn_ - dr))).reshape(Q_LORA_RANK, nh * HEAD_PAD).astype(BF16)
    wkv = w_ukv.reshape(KV_LORA_RANK, nh, dn_ + dv)
    wk = jnp.pad(wkv[:, :, :dn_], ((0, 0), (0, 0), (0, HEAD_PAD - dn_))).reshape(KV_LORA_RANK, nh * HEAD_PAD).astype(BF16)
    wvt = wkv[:, :, dn_:].reshape(KV_LORA_RANK, nh * dv).T.astype(BF16)
    return (g.reshape(1, d), w_dn, q_norm.reshape(1, -1), wq, kv_norm.reshape(1, -1), wk, wvt) + tuple(tabs)


FLASH_HEADS = 4


def _flash_kernel(q_ref, k_ref, vt_ref, o_ref, sa_ref, sb_ref, acc_ref):
    tq = q_ref.shape[1]
    tk = vt_ref.shape[4]
    qi = pl.program_id(2)

    def scores(j, s_ref, c0=0):
        k2 = k_ref[0, pl.ds(pl.multiple_of(j * tk, tk), tk), :]
        for hh in range(FLASH_HEADS):
            sl = slice(hh * HEAD_PAD, (hh + 1) * HEAD_PAD)
            s_ref[hh, :, c0:] = _dot_nt(k2[:, sl], q_ref[0, c0:, sl])

    def consume(j, s_ref, ms, stream, c0=0, diag=False):
        out = []
        for hh in range(FLASH_HEADS):
            st = s_ref[hh, :, c0:]
            if diag:
                key = lax.broadcasted_iota(I32, st.shape, 0)
                qry = lax.broadcasted_iota(I32, st.shape, 1)
                st = jnp.where(key <= qry, st, NEG_BIG)
            m = ms[hh][:, c0:]
            m_new = jnp.maximum(m, jnp.max(st, axis=0, keepdims=True))
            alpha = jnp.exp2(m - m_new)
            p = jnp.exp2(st - m_new).astype(BF16)
            acc_ref[stream, hh, :, c0:] = alpha * acc_ref[stream, hh, :, c0:] + _dot(vt_ref[0, hh, j], p)
            out.append(m_new if c0 == 0 else jnp.concatenate([ms[hh][:, :c0], m_new], axis=1))
        return tuple(out)

    acc_ref[...] = jnp.zeros_like(acc_ref)
    ms0 = tuple(jnp.full((1, tq), NEG_BIG, F32) for _ in range(2 * FLASH_HEADS))
    scores(0, sa_ref)

    def two_tiles(i, ms):
        scores(2 * i + 1, sb_ref)
        even = consume(2 * i, sa_ref, ms[:FLASH_HEADS], 0)
        scores(2 * i + 2, sa_ref)
        return even + consume(2 * i + 1, sb_ref, ms[FLASH_HEADS:], 1)

    assert tq == 2 * tk
    ms = lax.fori_loop(0, qi, two_tiles, ms0)
    scores(2 * qi + 1, sb_ref, c0=tk)
    even = consume(2 * qi, sa_ref, ms[:FLASH_HEADS], 0, diag=True)
    odd = consume(2 * qi + 1, sb_ref, ms[FLASH_HEADS:], 1, c0=tk, diag=True)
    heads = []
    for hh in range(FLASH_HEADS):
        m = jnp.maximum(even[hh], odd[hh])
        acc = jnp.exp2(even[hh] - m) * acc_ref[0, hh] + jnp.exp2(odd[hh] - m) * acc_ref[1, hh]
        heads.append(acc[0:V_HEAD_DIM, :] * (1.0 / acc[V_HEAD_DIM:V_HEAD_DIM + 1, :]))
    o_ref[0] = jnp.concatenate(heads, axis=0).T.astype(BF16)


def _flash(q, k, vt, tq=512):
    b, s, _ = q.shape
    _, nh, nblk, rows, tk = vt.shape
    return pl.pallas_call(
        _flash_kernel,
        grid=(b, nh // FLASH_HEADS, s // tq),
        in_specs=[
            pl.BlockSpec((1, tq, FLASH_HEADS * HEAD_PAD), lambda i, p, j: (i, j, p)),
            pl.BlockSpec((1, s, FLASH_HEADS * HEAD_PAD), lambda i, p, j: (i, 0, p)),
            pl.BlockSpec((1, FLASH_HEADS, nblk, rows, tk), lambda i, p, j: (i, p, 0, 0, 0)),
        ],
        out_specs=pl.BlockSpec((1, tq, FLASH_HEADS * V_HEAD_DIM), lambda i, p, j: (i, j, p)),
        out_shape=jax.ShapeDtypeStruct((b, s, nh * V_HEAD_DIM), BF16),
        scratch_shapes=[pltpu.VMEM((FLASH_HEADS, tk, tq), F32), pltpu.VMEM((FLASH_HEADS, tk, tq), F32),
                        pltpu.VMEM((2, FLASH_HEADS, rows, tq), F32)],
        compiler_params=_params("arbitrary", "arbitrary", "arbitrary"),
        name="flash",
    )(q, k, vt)


def kernel(x, mem, positions, norm_mix, conv_w_in, conv_b_in, conv_w_dw, conv_b_dw, conv_ln_g, conv_ln_b, conv_w_out, conv_b_out, mla_w_down, mla_q_norm, mla_w_uq, mla_kv_norm, mla_w_ukv, mla_w_o, norm_xa, norm_mem, xa_wq, xa_wkv, xa_wo, norm_ffn, moe_w_grp, moe_b_grp, moe_w_exp, moe_b_exp, moe_w_gate, moe_w_up, moe_w_down, final_norm):
    tabs = _rope_tables(positions)
    kv_mem = _mem_kv(mem, norm_mem, xa_wkv.astype(BF16))

    def ffn(h, i, final, mla=None):
        return _moe(h, norm_ffn[i], moe_w_grp[i], moe_b_grp[i], moe_w_exp[i], moe_b_exp[i],
                    moe_w_gate, moe_w_up, moe_w_down, i, final_norm, final, mla)

    h = _conv_mixer(x, norm_mix[0], conv_w_in[0], conv_b_in[0], conv_w_dw[0], conv_b_dw[0],
                    conv_ln_g[0], conv_ln_b[0], conv_w_out[0], conv_b_out[0])
    h = _xattn(h, norm_xa[0], xa_wq[0], kv_mem, 0, xa_wo[0])
    mla = _mla_operands(x.shape[-1], norm_mix[1], mla_w_down[0], mla_q_norm[0], mla_w_uq[0], mla_kv_norm[0],
                        mla_w_ukv[0], tabs)
    h, q, k, vt = ffn(h, 0, False, mla)
    attn = _flash(q, k, vt)
    h = _xattn(h, norm_xa[1], xa_wq[1], kv_mem, 1, xa_wo[1], prefix=(attn, mla_w_o[0]))
    return ffn(h, 1, True)
```

```python
import functools
import math

import jax
import jax.numpy as jnp
from jax import lax
from jax.experimental import pallas as pl
from jax.experimental.pallas import tpu as pltpu

F32 = jnp.float32
BF16 = jnp.bfloat16
I32 = jnp.int32

NORM_EPS = 1e-6
CONV_WIDTH = 31
CONV_HALO = 32
SUBLANES = 8
MLA_HEADS = 16
QK_NOPE_DIM = 64
QK_ROPE_DIM = 32
V_HEAD_DIM = 64
Q_LORA_RANK = 384
KV_LORA_RANK = 256
ROPE_THETA = 10000.0
HEAD_PAD = 128
XA_HEADS = 4
N_GROUPS = 4
EXPERTS_PER_GROUP = 8
N_EXPERTS = N_GROUPS * EXPERTS_PER_GROUP
TOP_K = 2
ROUTER_LANES = 128
NEG_BIG = -1e30
LOG2_E = 1.4426950408889634

VMEM_LIMIT = 56 * 1024 * 1024


def _rms(x, g):
    return x * lax.rsqrt(jnp.mean(x * x, axis=-1, keepdims=True) + NORM_EPS) * g


def _dot(a, b):
    return jnp.dot(a, b, preferred_element_type=F32)


def _dot_nt(a, b):
    return lax.dot_general(a, b, (((1,), (1,)), ((), ())), preferred_element_type=F32)


def _params(*sem):
    return pltpu.CompilerParams(dimension_semantics=sem, vmem_limit_bytes=VMEM_LIMIT)


def _const_spec(shape):
    return pl.BlockSpec(shape, lambda *_: (0,) * len(shape))


def _rope_kernel(pos_ref, freq_ref, cos_ref, sin_ref):
    ang = pos_ref[...].astype(F32) * freq_ref[...]
    cos_ref[...] = jnp.cos(ang)
    sin_ref[...] = jnp.sin(ang)


def _rope_tables(positions):
    half = QK_ROPE_DIM // 2
    t = positions.size
    inv_freq = 1.0 / (ROPE_THETA ** (jnp.arange(0, QK_ROPE_DIM, 2, dtype=F32) / QK_ROPE_DIM))
    rows = t * half // 128
    pos_rep = jnp.repeat(positions.reshape(-1), half).reshape(rows, 128)
    freq = jnp.tile(inv_freq, 128 // half).reshape(1, 128)
    cos, sin = pl.pallas_call(
        _rope_kernel,
        out_shape=(jax.ShapeDtypeStruct((rows, 128), F32),) * 2,
        name="rope_tables",
    )(pos_rep, freq)
    cos = cos.reshape(t, half)
    sin = sin.reshape(t, half)
    one = jnp.ones((t, QK_NOPE_DIM), F32)
    z16 = jnp.zeros((t, half), F32)
    z64 = jnp.zeros((t, QK_NOPE_DIM), F32)
    pad1 = jnp.ones((t, HEAD_PAD - QK_NOPE_DIM - QK_ROPE_DIM), F32)
    pad0 = jnp.zeros((t, HEAD_PAD - QK_NOPE_DIM - QK_ROPE_DIM), F32)
    tab_c = jnp.concatenate([one, cos, cos, pad1], axis=1)
    tab_a = jnp.concatenate([z64, sin, z16, pad0], axis=1)
    tab_b = jnp.concatenate([z64, z16, sin, pad0], axis=1)
    return tab_c, tab_a, tab_b


def _rope_apply(x, c, a, b):
    return x * c - pltpu.roll(x, HEAD_PAD - 16, 1) * a + pltpu.roll(x, 16, 1) * b


def _mem_kv_kernel(mem_ref, g_ref, w_ref, o_ref):
    mn = _rms(mem_ref[0], g_ref[0]).astype(BF16)
    o_ref[0, 0] = _dot(mn, w_ref[0]).astype(BF16)


def _mem_kv(mem, norm_mem, wkv):
    b, m, d = mem.shape
    l = norm_mem.shape[0]
    return pl.pallas_call(
        _mem_kv_kernel,
        grid=(l, b),
        in_specs=[
            pl.BlockSpec((1, m, d), lambda i, j: (j, 0, 0)),
            pl.BlockSpec((1, 1, d), lambda i, j: (i, 0, 0)),
            pl.BlockSpec((1, d, 2 * d), lambda i, j: (i, 0, 0)),
        ],
        out_specs=pl.BlockSpec((1, 1, m, 2 * d), lambda i, j: (i, j, 0, 0)),
        out_shape=jax.ShapeDtypeStruct((l, b, m, 2 * d), BF16),
        compiler_params=_params("arbitrary", "arbitrary"),
        name="mem_kv",
    )(mem, norm_mem.reshape(l, 1, d), wkv)


CONV_ROWS = 16
CONV_COLS = 4


def _conv_mixer_kernel(h_ref, g_ref, win_ref, bin_ref, wdw_ref, bdw_ref, lng_ref, lnb_ref,
                       wout_ref, bout_ref, o_ref, cb_ref, cv_ref):
    ts, d = h_ref.shape[1], h_ref.shape[2]

    @pl.when(pl.program_id(1) == 0)
    def _():
        cb_ref[0, 0:CONV_HALO, :] = jnp.zeros((CONV_HALO, d), F32)

    h = h_ref[0]
    hn = _rms(h, g_ref[...]).astype(BF16)
    u = _dot(hn, win_ref[...]) + bin_ref[...]
    cb_ref[0, CONV_HALO:CONV_HALO + ts, :] = u[:, :d] * jax.nn.sigmoid(u[:, d:])
    ncopy = ts + CONV_HALO - SUBLANES
    for s in range(1, SUBLANES):
        cb_ref[s, 0:ncopy, :] = cb_ref[0, s:s + ncopy, :]

    base = CONV_HALO - (CONV_WIDTH - 1)

    def rows(i, carry):
        r0 = pl.multiple_of(i * CONV_ROWS, CONV_ROWS)
        for cg in range(0, d, LANES * CONV_COLS):
            cols = [cg + LANES * c for c in range(CONV_COLS)]
            accs = [jnp.zeros((CONV_ROWS, LANES), F32) for _ in cols]
            for k in range(CONV_WIDTH):
                off = base + k
                row0 = pl.multiple_of(r0 + (off // SUBLANES) * SUBLANES, SUBLANES)
                for ci, c0 in enumerate(cols):
                    tap = cb_ref[off % SUBLANES, pl.ds(row0, CONV_ROWS), c0:c0 + LANES]
                    accs[ci] = accs[ci] + tap * wdw_ref[k:k + 1, c0:c0 + LANES]
            for ci, c0 in enumerate(cols):
                cv_ref[pl.ds(r0, CONV_ROWS), c0:c0 + LANES] = accs[ci]
        return carry

    lax.fori_loop(0, ts // CONV_ROWS, rows, 0)
    cb_ref[0, 0:CONV_HALO, :] = cb_ref[0, ts:ts + CONV_HALO, :]

    v = cv_ref[...] + bdw_ref[...]
    mu = jnp.mean(v, axis=-1, keepdims=True)
    vc = v - mu
    var = jnp.mean(vc * vc, axis=-1, keepdims=True)
    y = vc * lax.rsqrt(var + NORM_EPS) * lng_ref[...] + lnb_ref[...]
    y = (y * jax.nn.sigmoid(y)).astype(BF16)
    o_ref[0] = h + _dot(y, wout_ref[...]) + bout_ref[...]


def _conv_mixer(h, g, w_in, b_in, w_dw, b_dw, ln_g, ln_b, w_out, b_out, ts=256):
    b, s, d = h.shape
    row = lambda v: v.reshape(1, -1)
    return pl.pallas_call(
        _conv_mixer_kernel,
        grid=(b, s // ts),
        in_specs=[
            pl.BlockSpec((1, ts, d), lambda i, j: (i, j, 0)),
            _const_spec((1, d)), _const_spec((d, 2 * d)), _const_spec((1, 2 * d)),
            _const_spec((CONV_WIDTH, d)), _const_spec((1, d)), _const_spec((1, d)), _const_spec((1, d)),
            _const_spec((d, d)), _const_spec((1, d)),
        ],
        out_specs=pl.BlockSpec((1, ts, d), lambda i, j: (i, j, 0)),
        out_shape=jax.ShapeDtypeStruct((b, s, d), F32),
        scratch_shapes=[pltpu.VMEM((SUBLANES, ts + CONV_HALO, d), F32), pltpu.VMEM((ts, d), F32)],
        compiler_params=_params("arbitrary", "arbitrary"),
        name="conv_mixer",
    )(h, row(g), w_in.astype(BF16), row(b_in), w_dw, row(b_dw), row(ln_g), row(ln_b),
      w_out.astype(BF16), row(b_out))


def _xattn_kernel(*refs, with_prefix):
    refs, route_in, (o_ref, hn_ref, info_ref, cnt_ref, carry_ref) = refs[:-8], refs[-8:-5], refs[-5:]
    if with_prefix:
        h_ref, a_ref, wmo_ref, g_ref, wq_ref, kv_ref, wo_ref = refs
        h = h_ref[0] + _dot(a_ref[0], wmo_ref[...])
    else:
        h_ref, g_ref, wq_ref, kv_ref, wo_ref = refs
        h = h_ref[0]
    d = h.shape[-1]
    hd = d // XA_HEADS
    hn = _rms(h, g_ref[...]).astype(BF16)
    q = (_dot(hn, wq_ref[...]) * (1.0 / math.sqrt(hd))).astype(BF16)
    outs = []
    for hh in range(XA_HEADS):
        k = kv_ref[0, 0, :, hh * hd:(hh + 1) * hd]
        v = kv_ref[0, 0, :, d + hh * hd:d + (hh + 1) * hd]
        sc = _dot_nt(q[:, hh * hd:(hh + 1) * hd], k)
        p = jnp.exp(sc - jnp.max(sc, axis=-1, keepdims=True))
        inv = 1.0 / jnp.sum(p, axis=-1, keepdims=True)
        outs.append((_dot(p.astype(BF16), v) * inv).astype(BF16))
    o = jnp.concatenate(outs, axis=-1)
    out = h + _dot(o, wo_ref[...])
    o_ref[0] = out
    first = (pl.program_id(0) == 0) & (pl.program_id(1) == 0)
    _route_tile(out, *route_in, hn_ref, info_ref, cnt_ref, carry_ref, first)


def _xattn(h, g, wq, kv, layer, wo, route, prefix=None):
    b, s, d = h.shape
    m = kv.shape[2]
    ts = ROUTE_TILE
    nblk = s // ts
    slab = d // LANES
    t = b * s
    tile = pl.BlockSpec((1, ts, d), lambda i, j: (i, j, 0))
    in_specs = [tile]
    args = [h]
    if prefix is not None:
        attn, w_mo = prefix
        in_specs += [tile, _const_spec((d, d))]
        args += [attn, w_mo.astype(BF16)]
    in_specs += [_const_spec((1, d)), _const_spec((d, d)),
                 pl.BlockSpec((1, 1, m, 2 * d), lambda i, j: (layer, i, 0, 0)), _const_spec((d, d))]
    args += [g.reshape(1, d), wq.astype(BF16), kv, wo.astype(BF16)]
    in_specs += [_const_spec((1, d)), _const_spec((ROUTER_ROWS, d)), _const_spec((ROUTER_ROWS, ROUTER_LANES))]
    args += list(route)
    return pl.pallas_call(
        functools.partial(_xattn_kernel, with_prefix=prefix is not None),
        grid=(b, nblk),
        in_specs=in_specs,
        out_specs=(tile,
                   pl.BlockSpec((ts * slab, LANES), lambda i, j: (i * nblk + j, 0)),
                   pl.BlockSpec((SUBLANES, ts), lambda i, j: (0, i * nblk + j)),
                   _const_spec((N_EXPERTS, ROUTER_LANES))),
        out_shape=(jax.ShapeDtypeStruct((b, s, d), F32), jax.ShapeDtypeStruct((t * slab, LANES), F32),
                   jax.ShapeDtypeStruct((SUBLANES, t), F32), jax.ShapeDtypeStruct((N_EXPERTS, ROUTER_LANES), F32)),
        scratch_shapes=[pltpu.VMEM((N_EXPERTS, ROUTER_LANES), F32)],
        compiler_params=_params("arbitrary", "arbitrary"),
        name="xattn",
    )(*args)


ROUTE_TILE = 512
EXPERT_TILE = 512
COMBINE_TILE = 512
DISPATCH_TILE = 512
LANES = 128


def _load_slabs(ref, rows, d):
    n = d // LANES
    return jnp.concatenate([ref[pl.ds(j, rows, stride=n), :] for j in range(n)], axis=1)


def _store_slabs(ref, val, accumulate=False):
    rows, d = val.shape
    n = d // LANES
    for j in range(n):
        chunk = val[:, j * LANES:(j + 1) * LANES]
        if accumulate:
            chunk = chunk + ref[pl.ds(j, rows, stride=n), :]
        ref[pl.ds(j, rows, stride=n), :] = chunk


INFO_E1, INFO_E2, INFO_R1, INFO_R2, INFO_W1, INFO_W2 = range(6)
ROUTER_EXPERT_ROW = SUBLANES
ROUTER_ROWS = ROUTER_EXPERT_ROW + N_EXPERTS + SUBLANES


def _route_tile(h, g_ref, wrt_ref, brt_ref, hn_ref, info_ref, cnt_ref, carry_ref, first):
    tm = h.shape[0]
    assert EXPERTS_PER_GROUP == SUBLANES

    @pl.when(first)
    def _():
        carry_ref[...] = jnp.zeros_like(carry_ref)

    hn = _rms(h, g_ref[...])
    _store_slabs(hn_ref, hn)
    logits = lax.dot_general(wrt_ref[...], hn, (((1,), (1,)), ((), ())), preferred_element_type=F32,
                             precision=lax.Precision.HIGHEST) + brt_ref[:, 0:1]
    sub = lax.broadcasted_iota(I32, (SUBLANES, tm), 0)
    gl = jnp.where(sub < N_GROUPS, logits[0:SUBLANES], NEG_BIG)
    gmax = jnp.max(gl, axis=0, keepdims=True)
    gsum = jnp.sum(jnp.exp(gl - gmax), axis=0, keepdims=True)
    g_val = 1.0 / gsum
    g_idx = jnp.min(jnp.where(gl == gmax, sub, SUBLANES), axis=0, keepdims=True)
    el = logits[ROUTER_EXPERT_ROW:ROUTER_EXPERT_ROW + SUBLANES]
    for grp in range(1, N_GROUPS):
        lo = ROUTER_EXPERT_ROW + grp * SUBLANES
        el = jnp.where(g_idx == grp, logits[lo:lo + SUBLANES], el)
    m1 = jnp.max(el, axis=0, keepdims=True)
    esum = jnp.sum(jnp.exp(el - m1), axis=0, keepdims=True)
    i1 = jnp.min(jnp.where(el == m1, sub, SUBLANES), axis=0, keepdims=True)
    el2 = jnp.where(sub == i1, NEG_BIG, el)
    m2 = jnp.max(el2, axis=0, keepdims=True)
    i2 = jnp.min(jnp.where(el2 == m2, sub, SUBLANES), axis=0, keepdims=True)
    p1 = 1.0 / esum
    p2 = jnp.exp(m2 - m1) / esum
    w1 = g_val * (p1 / (p1 + p2))
    w2 = g_val * (p2 / (p1 + p2))
    e1 = g_idx * EXPERTS_PER_GROUP + i1
    e2 = g_idx * EXPERTS_PER_GROUP + i2
    erow = lax.broadcasted_iota(I32, (N_EXPERTS, tm), 0)
    hit = jnp.where((erow == e1) | (erow == e2), 1.0, 0.0)
    r = lax.broadcasted_iota(I32, (tm, tm), 0)
    c = lax.broadcasted_iota(I32, (tm, tm), 1)
    before = jnp.where(r < c, 1.0, 0.0).astype(BF16)
    tot = _dot(hit.astype(BF16), before) + carry_ref[:, 0:1]
    rank1 = jnp.sum(jnp.where(erow == e1, tot, 0.0), axis=0, keepdims=True)
    rank2 = jnp.sum(jnp.where(erow == e2, tot, 0.0), axis=0, keepdims=True)
    carry_ref[...] = carry_ref[...] + jnp.sum(hit, axis=1, keepdims=True)
    cnt_ref[...] = carry_ref[...]
    info = jnp.zeros((SUBLANES, tm), F32)
    for row, val in ((INFO_E1, e1.astype(F32)), (INFO_E2, e2.astype(F32)), (INFO_R1, rank1),
                     (INFO_R2, rank2), (INFO_W1, w1), (INFO_W2, w2)):
        info = jnp.where(sub == row, val, info)
    info_ref[...] = info


DISPATCH_SLOTS = 3


def _dispatch_kernel(p1_ref, p2_ref, hn_ref, xs_ref, buf, lsem, ssem):
    i = pl.program_id(0)
    n = pl.num_programs(0)
    slab = xs_ref.shape[1]
    rows = buf.shape[1]
    tm = rows // slab
    slot = i % DISPATCH_SLOTS

    def load(step):
        sl = step % DISPATCH_SLOTS
        return pltpu.make_async_copy(hn_ref.at[pl.ds(step * rows, rows)], buf.at[sl], lsem.at[sl])

    def wait_scatters(sl):
        cp = pltpu.make_async_copy(buf.at[sl], hn_ref.at[pl.ds(0, rows)], ssem.at[sl])
        for _ in range(TOP_K):
            cp.wait()

    @pl.when(i == 0)
    def _():
        load(0).start()

    @pl.when(i + 1 < n)
    def _():
        load(i + 1).start()

    load(i).wait()

    def tok(r, carry):
        tt = i * tm + r
        src = buf.at[slot, pl.ds(pl.multiple_of(r * slab, slab), slab)]
        pltpu.make_async_copy(src, xs_ref.at[p1_ref[tt]], ssem.at[slot]).start()
        pltpu.make_async_copy(src, xs_ref.at[p2_ref[tt]], ssem.at[slot]).start()
        return carry

    lax.fori_loop(0, tm, tok, 0, unroll=8)

    @pl.when(i > 0)
    def _():
        wait_scatters((i - 1) % DISPATCH_SLOTS)

    @pl.when(i == n - 1)
    def _():
        wait_scatters(slot)


def _expert_kernel(ve_ref, vt_ref, lo_ref, hi_ref, nv_ref, x_ref, wg_ref, wu_ref, wd_ref, y_ref,
                   wgu_s, wd_s, gu_s):
    v = pl.program_id(0)
    d, f = wg_ref.shape[2], wg_ref.shape[3]
    tm = x_ref.shape[0] * LANES // d

    @pl.when(v < nv_ref[0])
    def _():
        pv = jnp.maximum(v - 1, 0)

        @pl.when((v == 0) | (ve_ref[v] != ve_ref[pv]))
        def _():
            wgu_s[:, :f] = wg_ref[0, 0].astype(BF16)
            wgu_s[:, f:] = wu_ref[0, 0].astype(BF16)
            wd_s[...] = wd_ref[0, 0].astype(BF16)

        half = tm // 2
        slab = d // LANES

        @pl.when((v == 0) | (vt_ref[v] != vt_ref[pv]))
        def _():
            y_ref[...] = jnp.zeros_like(y_ref)

        for hh in range(2):
            x = _load_slabs(x_ref.at[pl.ds(hh * half * slab, half * slab)], half, d)
            gu_s[hh] = _dot(x.astype(BF16), wgu_s[...])
        for hh in range(2):
            gate = gu_s[hh, :, :f]
            he = (gate * jax.nn.sigmoid(gate) * gu_s[hh, :, f:]).astype(BF16)
            row = lax.broadcasted_iota(I32, (half, 1), 0) + hh * half
            y = jnp.where((row >= lo_ref[v]) & (row < hi_ref[v]), _dot(he, wd_s[...]), 0.0)
            _store_slabs(y_ref.at[pl.ds(hh * half * slab, half * slab)], y, accumulate=True)


def _combine_kernel(p1_ref, p2_ref, h_ref, info_ref, fin_ref, ys_ref, ys2d_ref, *rest, final, with_mla):
    if with_mla:
        mla_in, (o_ref, q_ref, k_ref, vt_ref, buf, sem) = rest[:10], rest[10:]
    else:
        o_ref, buf, sem = rest
    i = pl.program_id(0)
    n = pl.num_programs(0)
    tm, d = h_ref.shape
    slab = ys_ref.shape[1]
    slot = i % 2

    def issue(step, sl):
        def tok(r, carry):
            tt = step * tm + r
            dst = pl.ds(pl.multiple_of(r * slab, slab), slab)
            pltpu.make_async_copy(ys_ref.at[p1_ref[tt]], buf.at[sl, 0, dst], sem.at[sl]).start()
            pltpu.make_async_copy(ys_ref.at[p2_ref[tt]], buf.at[sl, 1, dst], sem.at[sl]).start()
            return carry

        lax.fori_loop(0, tm, tok, 0, unroll=8)

    @pl.when(i == 0)
    def _():
        issue(0, 0)

    @pl.when(i + 1 < n)
    def _():
        issue(i + 1, 1 - slot)

    for k in range(TOP_K):
        pltpu.make_async_copy(ys2d_ref.at[pl.ds(0, tm * slab)], buf.at[slot, k], sem.at[slot]).wait()
    info = jnp.concatenate([info_ref[...], jnp.zeros((ROUTER_LANES - SUBLANES, tm), F32)], axis=0).T
    out = (h_ref[...] + info[:, INFO_W1:INFO_W1 + 1] * _load_slabs(buf.at[slot, 0], tm, d)
           + info[:, INFO_W2:INFO_W2 + 1] * _load_slabs(buf.at[slot, 1], tm, d))
    if final:
        out = _rms(out, fin_ref[...])
    o_ref[...] = out
    if with_mla:
        _mla_tile(out, *mla_in, q_ref, k_ref, vt_ref)


def _route_operands(g, w_grp, b_grp, w_exp, b_exp):
    d = g.shape[-1]
    zrow = lambda n: jnp.zeros((n, d), F32)
    wrt = jnp.concatenate([w_grp.T, zrow(ROUTER_EXPERT_ROW - N_GROUPS), w_exp.T,
                           zrow(ROUTER_ROWS - ROUTER_EXPERT_ROW - N_EXPERTS)], axis=0)
    brt = jnp.concatenate([b_grp, jnp.zeros((ROUTER_EXPERT_ROW - N_GROUPS,), F32), b_exp,
                           jnp.zeros((ROUTER_ROWS - ROUTER_EXPERT_ROW - N_EXPERTS,), F32)])
    brt = jnp.broadcast_to(brt[:, None], (ROUTER_ROWS, ROUTER_LANES))
    return g.reshape(1, d), wrt, brt


def _moe(h, routed, w_gate, w_up, w_down, layer, final_norm, final, mla=None):
    hn, info, cnt = routed
    b, s, d = h.shape
    t = b * s
    f = w_gate.shape[-1]
    slab = d // LANES
    assert slab == SUBLANES
    h2 = h.reshape(t, d)

    tm = EXPERT_TILE
    n_rows = TOP_K * t
    n_visits = n_rows // tm + N_EXPERTS - 1
    counts = cnt[:, 0].astype(I32)
    row_end = jnp.cumsum(counts)
    row_start = row_end - counts
    first_tile = row_start // tm
    e_visits = jnp.where(counts > 0, (row_end - 1) // tm - first_tile + 1, 0)
    v_end = jnp.cumsum(e_visits)
    v_start = v_end - e_visits
    n_v = v_end[-1:]
    vis = jnp.minimum(jnp.arange(n_visits, dtype=I32), n_v - 1)
    v_e = jnp.sum(vis[:, None] >= v_end[None, :], axis=1).astype(I32)
    v_tile = (first_tile[v_e] + vis - v_start[v_e]).astype(I32)
    v_lo = (jnp.maximum(row_start[v_e], v_tile * tm) - v_tile * tm).astype(I32)
    v_hi = (jnp.minimum(row_end[v_e], (v_tile + 1) * tm) - v_tile * tm).astype(I32)
    eid = jnp.arange(N_EXPERTS, dtype=I32)[:, None]
    start_of = lambda e: jnp.sum(jnp.where(e.astype(I32)[None, :] == eid, row_start[:, None], 0), axis=0)
    pos1 = start_of(info[INFO_E1]) + info[INFO_R1].astype(I32)
    pos2 = start_of(info[INFO_E2]) + info[INFO_R2].astype(I32)

    xs = pl.pallas_call(
        _dispatch_kernel,
        grid_spec=pltpu.PrefetchScalarGridSpec(
            num_scalar_prefetch=2, grid=(t // DISPATCH_TILE,),
            in_specs=[pl.BlockSpec(memory_space=pl.ANY)],
            out_specs=pl.BlockSpec(memory_space=pl.ANY),
            scratch_shapes=[pltpu.VMEM((DISPATCH_SLOTS, DISPATCH_TILE * slab, LANES), F32),
                            pltpu.SemaphoreType.DMA((DISPATCH_SLOTS,)),
                            pltpu.SemaphoreType.DMA((DISPATCH_SLOTS,))],
        ),
        out_shape=jax.ShapeDtypeStruct((n_rows, slab, LANES), F32),
        compiler_params=_params("arbitrary"),
        name="moe_dispatch",
    )(pos1, pos2, hn)

    by_tile = lambda v, ve, vt, lo, hi, nv: (vt[v], 0)
    by_expert = lambda v, ve, vt, lo, hi, nv: (layer, ve[v], 0, 0)
    ys = pl.pallas_call(
        _expert_kernel,
        grid_spec=pltpu.PrefetchScalarGridSpec(
            num_scalar_prefetch=5, grid=(n_visits,),
            in_specs=[
                pl.BlockSpec((tm * slab, LANES), by_tile),
                pl.BlockSpec((1, 1, d, f), by_expert),
                pl.BlockSpec((1, 1, d, f), by_expert),
                pl.BlockSpec((1, 1, f, d), by_expert),
            ],
            out_specs=pl.BlockSpec((tm * slab, LANES), by_tile),
            scratch_shapes=[pltpu.VMEM((d, 2 * f), BF16), pltpu.VMEM((f, d), BF16),
                            pltpu.VMEM((2, tm // 2, 2 * f), F32)],
        ),
        out_shape=jax.ShapeDtypeStruct((n_rows * slab, LANES), F32),
        compiler_params=_params("arbitrary"),
        name="moe_experts",
    )(v_e, v_tile, v_lo, v_hi, n_v, xs.reshape(n_rows * slab, LANES), w_gate, w_up, w_down)

    tc = COMBINE_TILE
    tok = lambda w: pl.BlockSpec((tc, w), lambda i, p1, p2: (i, 0))
    in_specs = [tok(d), pl.BlockSpec((SUBLANES, tc), lambda i, p1, p2: (0, i)),
                pl.BlockSpec((1, d), lambda i, p1, p2: (0, 0)),
                pl.BlockSpec(memory_space=pl.ANY), pl.BlockSpec(memory_space=pl.ANY)]
    args = [h2, info, final_norm.reshape(1, d), ys.reshape(n_rows, slab, LANES), ys]
    out_specs = tok(d)
    out_shape = jax.ShapeDtypeStruct((t, d), F32)
    if mla is not None:
        assert tc % MLA_TILE == 0 and s % tc == 0
        nblk = s // tc
        nsub = tc // MLA_TILE
        width = MLA_HEADS * HEAD_PAD
        in_specs += [pl.BlockSpec(w.shape, lambda i, p1, p2, nd=w.ndim: (0,) * nd) for w in mla[:7]]
        in_specs += [tok(HEAD_PAD)] * 3
        args += list(mla)
        out_specs = (out_specs, tok(width), tok(width),
                     pl.BlockSpec((1, MLA_HEADS, nsub, VT_ROWS, MLA_TILE),
                                  lambda i, p1, p2: (i // nblk, 0, i % nblk, 0, 0)))
        out_shape = (out_shape, jax.ShapeDtypeStruct((t, width), BF16), jax.ShapeDtypeStruct((t, width), BF16),
                     jax.ShapeDtypeStruct((b, MLA_HEADS, s // MLA_TILE, VT_ROWS, MLA_TILE), BF16))
    res = pl.pallas_call(
        functools.partial(_combine_kernel, final=final, with_mla=mla is not None),
        grid_spec=pltpu.PrefetchScalarGridSpec(
            num_scalar_prefetch=2, grid=(t // tc,),
            in_specs=in_specs,
            out_specs=out_specs,
            scratch_shapes=[pltpu.VMEM((2, TOP_K, tc * slab, LANES), F32), pltpu.SemaphoreType.DMA((2,))],
        ),
        out_shape=out_shape,
        compiler_params=_params("arbitrary"),
        name="moe_combine",
    )(pos1, pos2, *args)
    if mla is None:
        return res.reshape(b, s, d)
    out, q, k, vt = res
    return out.reshape(b, s, d), q.reshape(b, s, -1), k.reshape(b, s, -1), vt


MLA_TILE = 256
VT_ROWS = V_HEAD_DIM + 16


def _mla_tile(h, g_ref, wdn_ref, qn_ref, wuq_ref, kvn_ref, wuk_ref, wuvt_ref, tc_ref, ta_ref, tb_ref,
              q_ref, k_ref, vt_ref):
    hn = _rms(h, g_ref[...]).astype(BF16)
    dn = _dot(hn, wdn_ref[...])
    c, a, b = tc_ref[...], ta_ref[...], tb_ref[...]
    scale = LOG2_E / math.sqrt(QK_NOPE_DIM + QK_ROPE_DIM)
    kv_lo = Q_LORA_RANK
    kr_lo = Q_LORA_RANK + KV_LORA_RANK
    cq = _rms(dn[:, :kv_lo], qn_ref[...]).astype(BF16)
    q = _dot(cq, wuq_ref[...])
    for hh in range(MLA_HEADS):
        sl = slice(hh * HEAD_PAD, (hh + 1) * HEAD_PAD)
        q_ref[:, sl] = (_rope_apply(q[:, sl], c, a, b) * scale).astype(BF16)
    ckv = _rms(dn[:, kv_lo:kr_lo], kvn_ref[...]).astype(BF16)
    kn = _dot(ckv, wuk_ref[...])
    kr = _rope_apply(dn[:, kr_lo:kr_lo + HEAD_PAD], c, a, b)
    for hh in range(MLA_HEADS):
        sl = slice(hh * HEAD_PAD, (hh + 1) * HEAD_PAD)
        k_ref[:, sl] = (kn[:, sl] + kr).astype(BF16)
    vt = _dot_nt(wuvt_ref[...], ckv).astype(BF16)
    tk = vt_ref.shape[4]
    ones = jnp.ones((VT_ROWS - V_HEAD_DIM, tk), BF16)
    for hh in range(MLA_HEADS):
        for sb in range(vt_ref.shape[2]):
            vt_ref[0, hh, sb, 0:V_HEAD_DIM, :] = vt[hh * V_HEAD_DIM:(hh + 1) * V_HEAD_DIM, sb * tk:(sb + 1) * tk]
            vt_ref[0, hh, sb, V_HEAD_DIM:VT_ROWS, :] = ones


def _mla_operands(d, g, w_down, q_norm, w_uq, kv_norm, w_ukv, tabs):
    nh, dn_, dr, dv = MLA_HEADS, QK_NOPE_DIM, QK_ROPE_DIM, V_HEAD_DIM
    kr_lo = Q_LORA_RANK + KV_LORA_RANK
    w_dn = jnp.concatenate([
        w_down[:, :kr_lo], jnp.zeros((d, dn_), F32), w_down[:, kr_lo:],
        jnp.zeros((d, HEAD_PAD - dn_ - dr), F32)], axis=1).astype(BF16)
    wq = w_uq.reshape(Q_LORA_RANK, nh, dn_ + dr)
    wq = jnp.pad(wq, ((0, 0), (0, 0), (0, HEAD_PAD - dn_ - dr))).reshape(Q_LORA_RANK, nh * HEAD_PAD).astype(BF16)
    wkv = w_ukv.reshape(KV_LORA_RANK, nh, dn_ + dv)
    wk = jnp.pad(wkv[:, :, :dn_], ((0, 0), (0, 0), (0, HEAD_PAD - dn_))).reshape(KV_LORA_RANK, nh * HEAD_PAD).astype(BF16)
    wvt = wkv[:, :, dn_:].reshape(KV_LORA_RANK, nh * dv).T.astype(BF16)
    return (g.reshape(1, d), w_dn, q_norm.reshape(1, -1), wq, kv_norm.reshape(1, -1), wk, wvt) + tuple(tabs)


FLASH_HEADS = 4


def _flash_kernel(q_ref, k_ref, vt_ref, o_ref, sa_ref, sb_ref, acc_ref):
    tq = q_ref.shape[1]
    tk = vt_ref.shape[4]
    qi = pl.program_id(2)

    def scores(j, s_ref, c0=0):
        k2 = k_ref[0, pl.ds(pl.multiple_of(j * tk, tk), tk), :]
        for hh in range(FLASH_HEADS):
            sl = slice(hh * HEAD_PAD, (hh + 1) * HEAD_PAD)
            s_ref[hh, :, c0:] = _dot_nt(k2[:, sl], q_ref[0, c0:, sl])

    def consume(j, s_ref, ms, stream, c0=0, diag=False):
        out = []
        for hh in range(FLASH_HEADS):
            st = s_ref[hh, :, c0:]
            if diag:
                key = lax.broadcasted_iota(I32, st.shape, 0)
                qry = lax.broadcasted_iota(I32, st.shape, 1)
                st = jnp.where(key <= qry, st, NEG_BIG)
            m = ms[hh][:, c0:]
            m_new = jnp.maximum(m, jnp.max(st, axis=0, keepdims=True))
            alpha = jnp.exp2(m - m_new)
            p = jnp.exp2(st - m_new).astype(BF16)
            acc_ref[stream, hh, :, c0:] = alpha * acc_ref[stream, hh, :, c0:] + _dot(vt_ref[0, hh, j], p)
            out.append(m_new if c0 == 0 else jnp.concatenate([ms[hh][:, :c0], m_new], axis=1))
        return tuple(out)

    acc_ref[...] = jnp.zeros_like(acc_ref)
    ms0 = tuple(jnp.full((1, tq), NEG_BIG, F32) for _ in range(2 * FLASH_HEADS))
    scores(0, sa_ref)

    def two_tiles(i, ms):
        scores(2 * i + 1, sb_ref)
        even = consume(2 * i, sa_ref, ms[:FLASH_HEADS], 0)
        scores(2 * i + 2, sa_ref)
        return even + consume(2 * i + 1, sb_ref, ms[FLASH_HEADS:], 1)

    assert tq == 2 * tk
    ms = lax.fori_loop(0, qi, two_tiles, ms0)
    scores(2 * qi + 1, sb_ref, c0=tk)
    even = consume(2 * qi, sa_ref, ms[:FLASH_HEADS], 0, diag=True)
    odd = consume(2 * qi + 1, sb_ref, ms[FLASH_HEADS:], 1, c0=tk, diag=True)
    heads = []
    for hh in range(FLASH_HEADS):
        m = jnp.maximum(even[hh], odd[hh])
        acc = jnp.exp2(even[hh] - m) * acc_ref[0, hh] + jnp.exp2(odd[hh] - m) * acc_ref[1, hh]
        heads.append(acc[0:V_HEAD_DIM, :] * (1.0 / acc[V_HEAD_DIM:V_HEAD_DIM + 1, :]))
    o_ref[0] = jnp.concatenate(heads, axis=0).T.astype(BF16)


def _flash(q, k, vt, tq=512):
    b, s, _ = q.shape
    _, nh, nblk, rows, tk = vt.shape
    return pl.pallas_call(
        _flash_kernel,
        grid=(b, nh // FLASH_HEADS, s // tq),
        in_specs=[
            pl.BlockSpec((1, tq, FLASH_HEADS * HEAD_PAD), lambda i, p, j: (i, j, p)),
            pl.BlockSpec((1, s, FLASH_HEADS * HEAD_PAD), lambda i, p, j: (i, 0, p)),
            pl.BlockSpec((1, FLASH_HEADS, nblk, rows, tk), lambda i, p, j: (i, p, 0, 0, 0)),
        ],
        out_specs=pl.BlockSpec((1, tq, FLASH_HEADS * V_HEAD_DIM), lambda i, p, j: (i, j, p)),
        out_shape=jax.ShapeDtypeStruct((b, s, nh * V_HEAD_DIM), BF16),
        scratch_shapes=[pltpu.VMEM((FLASH_HEADS, tk, tq), F32), pltpu.VMEM((FLASH_HEADS, tk, tq), F32),
                        pltpu.VMEM((2, FLASH_HEADS, rows, tq), F32)],
        compiler_params=_params("arbitrary", "arbitrary", "arbitrary"),
        name="flash",
    )(q, k, vt)


def kernel(x, mem, positions, norm_mix, conv_w_in, conv_b_in, conv_w_dw, conv_b_dw, conv_ln_g, conv_ln_b, conv_w_out, conv_b_out, mla_w_down, mla_q_norm, mla_w_uq, mla_kv_norm, mla_w_ukv, mla_w_o, norm_xa, norm_mem, xa_wq, xa_wkv, xa_wo, norm_ffn, moe_w_grp, moe_b_grp, moe_w_exp, moe_b_exp, moe_w_gate, moe_w_up, moe_w_down, final_norm):
    tabs = _rope_tables(positions)
    kv_mem = _mem_kv(mem, norm_mem, xa_wkv.astype(BF16))

    def route(i):
        return _route_operands(norm_ffn[i], moe_w_grp[i], moe_b_grp[i], moe_w_exp[i], moe_b_exp[i])

    def ffn(h, routed, i, final, mla=None):
        return _moe(h, routed, moe_w_gate, moe_w_up, moe_w_down, i, final_norm, final, mla)

    h = _conv_mixer(x, norm_mix[0], conv_w_in[0], conv_b_in[0], conv_w_dw[0], conv_b_dw[0],
                    conv_ln_g[0], conv_ln_b[0], conv_w_out[0], conv_b_out[0])
    h, *routed = _xattn(h, norm_xa[0], xa_wq[0], kv_mem, 0, xa_wo[0], route(0))
    mla = _mla_operands(x.shape[-1], norm_mix[1], mla_w_down[0], mla_q_norm[0], mla_w_uq[0], mla_kv_norm[0],
                        mla_w_ukv[0], tabs)
    h, q, k, vt = ffn(h, routed, 0, False, mla)
    attn = _flash(q, k, vt)
    h, *routed = _xattn(h, norm_xa[1], xa_wq[1], kv_mem, 1, xa_wo[1], route(1), prefix=(attn, mla_w_o[0]))
    return ffn(h, routed, 1, True)
```

```python
import functools
import math

import jax
import jax.numpy as jnp
from jax import lax
from jax.experimental import pallas as pl
from jax.experimental.pallas import tpu as pltpu

F32 = jnp.float32
BF16 = jnp.bfloat16
I32 = jnp.int32

NORM_EPS = 1e-6
CONV_WIDTH = 31
CONV_HALO = 32
SUBLANES = 8
MLA_HEADS = 16
QK_NOPE_DIM = 64
QK_ROPE_DIM = 32
V_HEAD_DIM = 64
Q_LORA_RANK = 384
KV_LORA_RANK = 256
ROPE_THETA = 10000.0
HEAD_PAD = 128
XA_HEADS = 4
N_GROUPS = 4
EXPERTS_PER_GROUP = 8
N_EXPERTS = N_GROUPS * EXPERTS_PER_GROUP
TOP_K = 2
ROUTER_LANES = 128
NEG_BIG = -1e30
LOG2_E = 1.4426950408889634

VMEM_LIMIT = 56 * 1024 * 1024


def _rms(x, g):
    return x * lax.rsqrt(jnp.mean(x * x, axis=-1, keepdims=True) + NORM_EPS) * g


def _dot(a, b):
    return jnp.dot(a, b, preferred_element_type=F32)


def _dot_nt(a, b):
    return lax.dot_general(a, b, (((1,), (1,)), ((), ())), preferred_element_type=F32)


def _params(*sem):
    return pltpu.CompilerParams(dimension_semantics=sem, vmem_limit_bytes=VMEM_LIMIT)


def _const_spec(shape):
    return pl.BlockSpec(shape, lambda *_: (0,) * len(shape))


def _rope_kernel(pos_ref, freq_ref, cos_ref, sin_ref):
    ang = pos_ref[...].astype(F32) * freq_ref[...]
    cos_ref[...] = jnp.cos(ang)
    sin_ref[...] = jnp.sin(ang)


def _rope_tables(positions):
    half = QK_ROPE_DIM // 2
    t = positions.size
    inv_freq = 1.0 / (ROPE_THETA ** (jnp.arange(0, QK_ROPE_DIM, 2, dtype=F32) / QK_ROPE_DIM))
    rows = t * half // 128
    pos_rep = jnp.repeat(positions.reshape(-1), half).reshape(rows, 128)
    freq = jnp.tile(inv_freq, 128 // half).reshape(1, 128)
    cos, sin = pl.pallas_call(
        _rope_kernel,
        out_shape=(jax.ShapeDtypeStruct((rows, 128), F32),) * 2,
        name="rope_tables",
    )(pos_rep, freq)
    cos = cos.reshape(t, half)
    sin = sin.reshape(t, half)
    one = jnp.ones((t, QK_NOPE_DIM), F32)
    z16 = jnp.zeros((t, half), F32)
    z64 = jnp.zeros((t, QK_NOPE_DIM), F32)
    pad1 = jnp.ones((t, HEAD_PAD - QK_NOPE_DIM - QK_ROPE_DIM), F32)
    pad0 = jnp.zeros((t, HEAD_PAD - QK_NOPE_DIM - QK_ROPE_DIM), F32)
    tab_c = jnp.concatenate([one, cos, cos, pad1], axis=1)
    tab_a = jnp.concatenate([z64, sin, z16, pad0], axis=1)
    tab_b = jnp.concatenate([z64, z16, sin, pad0], axis=1)
    return tab_c, tab_a, tab_b


def _rope_apply(x, c, a, b):
    return x * c - pltpu.roll(x, HEAD_PAD - 16, 1) * a + pltpu.roll(x, 16, 1) * b


def _mem_kv_kernel(mem_ref, g_ref, w_ref, o_ref):
    mn = _rms(mem_ref[0], g_ref[0]).astype(BF16)
    o_ref[0, 0] = _dot(mn, w_ref[0]).astype(BF16)


def _mem_kv(mem, norm_mem, wkv):
    b, m, d = mem.shape
    l = norm_mem.shape[0]
    return pl.pallas_call(
        _mem_kv_kernel,
        grid=(l, b),
        in_specs=[
            pl.BlockSpec((1, m, d), lambda i, j: (j, 0, 0)),
            pl.BlockSpec((1, 1, d), lambda i, j: (i, 0, 0)),
            pl.BlockSpec((1, d, 2 * d), lambda i, j: (i, 0, 0)),
        ],
        out_specs=pl.BlockSpec((1, 1, m, 2 * d), lambda i, j: (i, j, 0, 0)),
        out_shape=jax.ShapeDtypeStruct((l, b, m, 2 * d), BF16),
        compiler_params=_params("arbitrary", "arbitrary"),
        name="mem_kv",
    )(mem, norm_mem.reshape(l, 1, d), wkv)


CONV_ROWS = 16
CONV_COLS = 4


def _conv_mixer_kernel(h_ref, g_ref, win_ref, bin_ref, wdw_ref, bdw_ref, lng_ref, lnb_ref,
                       wout_ref, bout_ref, o_ref, cb_ref, cv_ref):
    ts, d = h_ref.shape[1], h_ref.shape[2]

    @pl.when(pl.program_id(1) == 0)
    def _():
        cb_ref[0, 0:CONV_HALO, :] = jnp.zeros((CONV_HALO, d), F32)

    h = h_ref[0]
    hn = _rms(h, g_ref[...]).astype(BF16)
    u = _dot(hn, win_ref[...]) + bin_ref[...]
    cb_ref[0, CONV_HALO:CONV_HALO + ts, :] = u[:, :d] * jax.nn.sigmoid(u[:, d:])
    ncopy = ts + CONV_HALO - SUBLANES
    for s in range(1, SUBLANES):
        cb_ref[s, 0:ncopy, :] = cb_ref[0, s:s + ncopy, :]

    base = CONV_HALO - (CONV_WIDTH - 1)

    def rows(i, carry):
        r0 = pl.multiple_of(i * CONV_ROWS, CONV_ROWS)
        for cg in range(0, d, LANES * CONV_COLS):
            cols = [cg + LANES * c for c in range(CONV_COLS)]
            accs = [jnp.zeros((CONV_ROWS, LANES), F32) for _ in cols]
            for k in range(CONV_WIDTH):
                off = base + k
                row0 = pl.multiple_of(r0 + (off // SUBLANES) * SUBLANES, SUBLANES)
                for ci, c0 in enumerate(cols):
                    tap = cb_ref[off % SUBLANES, pl.ds(row0, CONV_ROWS), c0:c0 + LANES]
                    accs[ci] = accs[ci] + tap * wdw_ref[k:k + 1, c0:c0 + LANES]
            for ci, c0 in enumerate(cols):
                cv_ref[pl.ds(r0, CONV_ROWS), c0:c0 + LANES] = accs[ci]
        return carry

    lax.fori_loop(0, ts // CONV_ROWS, rows, 0)
    cb_ref[0, 0:CONV_HALO, :] = cb_ref[0, ts:ts + CONV_HALO, :]

    v = cv_ref[...] + bdw_ref[...]
    mu = jnp.mean(v, axis=-1, keepdims=True)
    vc = v - mu
    var = jnp.mean(vc * vc, axis=-1, keepdims=True)
    y = vc * lax.rsqrt(var + NORM_EPS) * lng_ref[...] + lnb_ref[...]
    y = (y * jax.nn.sigmoid(y)).astype(BF16)
    o_ref[0] = h + _dot(y, wout_ref[...]) + bout_ref[...]


def _conv_mixer(h, g, w_in, b_in, w_dw, b_dw, ln_g, ln_b, w_out, b_out, ts=512):
    b, s, d = h.shape
    row = lambda v: v.reshape(1, -1)
    return pl.pallas_call(
        _conv_mixer_kernel,
        grid=(b, s // ts),
        in_specs=[
            pl.BlockSpec((1, ts, d), lambda i, j: (i, j, 0)),
            _const_spec((1, d)), _const_spec((d, 2 * d)), _const_spec((1, 2 * d)),
            _const_spec((CONV_WIDTH, d)), _const_spec((1, d)), _const_spec((1, d)), _const_spec((1, d)),
            _const_spec((d, d)), _const_spec((1, d)),
        ],
        out_specs=pl.BlockSpec((1, ts, d), lambda i, j: (i, j, 0)),
        out_shape=jax.ShapeDtypeStruct((b, s, d), F32),
        scratch_shapes=[pltpu.VMEM((SUBLANES, ts + CONV_HALO, d), F32), pltpu.VMEM((ts, d), F32)],
        compiler_params=_params("arbitrary", "arbitrary"),
        name="conv_mixer",
    )(h, row(g), w_in.astype(BF16), row(b_in), w_dw, row(b_dw), row(ln_g), row(ln_b),
      w_out.astype(BF16), row(b_out))


def _xattn_kernel(*refs, with_prefix):
    refs, route_in, (o_ref, hn_ref, info_ref, cnt_ref, carry_ref) = refs[:-8], refs[-8:-5], refs[-5:]
    if with_prefix:
        h_ref, a_ref, wmo_ref, g_ref, wq_ref, kv_ref, wo_ref = refs
        h = h_ref[0] + _dot(a_ref[0], wmo_ref[...])
    else:
        h_ref, g_ref, wq_ref, kv_ref, wo_ref = refs
        h = h_ref[0]
    d = h.shape[-1]
    hd = d // XA_HEADS
    hn = _rms(h, g_ref[...]).astype(BF16)
    q = (_dot(hn, wq_ref[...]) * (1.0 / math.sqrt(hd))).astype(BF16)
    outs = []
    for hh in range(XA_HEADS):
        k = kv_ref[0, 0, :, hh * hd:(hh + 1) * hd]
        v = kv_ref[0, 0, :, d + hh * hd:d + (hh + 1) * hd]
        sc = _dot_nt(q[:, hh * hd:(hh + 1) * hd], k)
        p = jnp.exp(sc - jnp.max(sc, axis=-1, keepdims=True))
        inv = 1.0 / jnp.sum(p, axis=-1, keepdims=True)
        outs.append((_dot(p.astype(BF16), v) * inv).astype(BF16))
    o = jnp.concatenate(outs, axis=-1)
    out = h + _dot(o, wo_ref[...])
    o_ref[0] = out
    first = (pl.program_id(0) == 0) & (pl.program_id(1) == 0)
    _route_tile(out, *route_in, hn_ref, info_ref, cnt_ref, carry_ref, first)


def _xattn(h, g, wq, kv, layer, wo, route, prefix=None):
    b, s, d = h.shape
    m = kv.shape[2]
    ts = ROUTE_TILE
    nblk = s // ts
    slab = d // LANES
    t = b * s
    tile = pl.BlockSpec((1, ts, d), lambda i, j: (i, j, 0))
    in_specs = [tile]
    args = [h]
    if prefix is not None:
        attn, w_mo = prefix
        in_specs += [tile, _const_spec((d, d))]
        args += [attn, w_mo.astype(BF16)]
    in_specs += [_const_spec((1, d)), _const_spec((d, d)),
                 pl.BlockSpec((1, 1, m, 2 * d), lambda i, j: (layer, i, 0, 0)), _const_spec((d, d))]
    args += [g.reshape(1, d), wq.astype(BF16), kv, wo.astype(BF16)]
    in_specs += [_const_spec((1, d)), _const_spec((ROUTER_ROWS, d)), _const_spec((ROUTER_ROWS, ROUTER_LANES))]
    args += list(route)
    return pl.pallas_call(
        functools.partial(_xattn_kernel, with_prefix=prefix is not None),
        grid=(b, nblk),
        in_specs=in_specs,
        out_specs=(tile,
                   pl.BlockSpec((ts * slab, LANES), lambda i, j: (i * nblk + j, 0)),
                   pl.BlockSpec((SUBLANES, ts), lambda i, j: (0, i * nblk + j)),
                   _const_spec((N_EXPERTS, ROUTER_LANES))),
        out_shape=(jax.ShapeDtypeStruct((b, s, d), F32), jax.ShapeDtypeStruct((t * slab, LANES), F32),
                   jax.ShapeDtypeStruct((SUBLANES, t), F32), jax.ShapeDtypeStruct((N_EXPERTS, ROUTER_LANES), F32)),
        scratch_shapes=[pltpu.VMEM((N_EXPERTS, ROUTER_LANES), F32)],
        compiler_params=_params("arbitrary", "arbitrary"),
        name="xattn",
    )(*args)


ROUTE_TILE = 512
EXPERT_TILE = 512
COMBINE_TILE = 512
DISPATCH_TILE = 512
LANES = 128


def _load_slabs(ref, rows, d):
    n = d // LANES
    return jnp.concatenate([ref[pl.ds(j, rows, stride=n), :] for j in range(n)], axis=1)


def _store_slabs(ref, val, accumulate=False):
    rows, d = val.shape
    n = d // LANES
    for j in range(n):
        chunk = val[:, j * LANES:(j + 1) * LANES]
        if accumulate:
            chunk = chunk + ref[pl.ds(j, rows, stride=n), :]
        ref[pl.ds(j, rows, stride=n), :] = chunk


INFO_E1, INFO_E2, INFO_R1, INFO_R2, INFO_W1, INFO_W2 = range(6)
ROUTER_EXPERT_ROW = SUBLANES
ROUTER_ROWS = ROUTER_EXPERT_ROW + N_EXPERTS + SUBLANES


def _route_tile(h, g_ref, wrt_ref, brt_ref, hn_ref, info_ref, cnt_ref, carry_ref, first):
    tm = h.shape[0]
    assert EXPERTS_PER_GROUP == SUBLANES

    @pl.when(first)
    def _():
        carry_ref[...] = jnp.zeros_like(carry_ref)

    hn = _rms(h, g_ref[...])
    _store_slabs(hn_ref, hn)
    logits = lax.dot_general(wrt_ref[...], hn, (((1,), (1,)), ((), ())), preferred_element_type=F32,
                             precision=lax.Precision.HIGHEST) + brt_ref[:, 0:1]
    sub = lax.broadcasted_iota(I32, (SUBLANES, tm), 0)
    gl = jnp.where(sub < N_GROUPS, logits[0:SUBLANES], NEG_BIG)
    gmax = jnp.max(gl, axis=0, keepdims=True)
    gsum = jnp.sum(jnp.exp(gl - gmax), axis=0, keepdims=True)
    g_val = 1.0 / gsum
    g_idx = jnp.min(jnp.where(gl == gmax, sub, SUBLANES), axis=0, keepdims=True)
    el = logits[ROUTER_EXPERT_ROW:ROUTER_EXPERT_ROW + SUBLANES]
    for grp in range(1, N_GROUPS):
        lo = ROUTER_EXPERT_ROW + grp * SUBLANES
        el = jnp.where(g_idx == grp, logits[lo:lo + SUBLANES], el)
    m1 = jnp.max(el, axis=0, keepdims=True)
    esum = jnp.sum(jnp.exp(el - m1), axis=0, keepdims=True)
    i1 = jnp.min(jnp.where(el == m1, sub, SUBLANES), axis=0, keepdims=True)
    el2 = jnp.where(sub == i1, NEG_BIG, el)
    m2 = jnp.max(el2, axis=0, keepdims=True)
    i2 = jnp.min(jnp.where(el2 == m2, sub, SUBLANES), axis=0, keepdims=True)
    p1 = 1.0 / esum
    p2 = jnp.exp(m2 - m1) / esum
    w1 = g_val * (p1 / (p1 + p2))
    w2 = g_val * (p2 / (p1 + p2))
    e1 = g_idx * EXPERTS_PER_GROUP + i1
    e2 = g_idx * EXPERTS_PER_GROUP + i2
    erow = lax.broadcasted_iota(I32, (N_EXPERTS, tm), 0)
    hit = jnp.where((erow == e1) | (erow == e2), 1.0, 0.0)
    r = lax.broadcasted_iota(I32, (tm, tm), 0)
    c = lax.broadcasted_iota(I32, (tm, tm), 1)
    before = jnp.where(r < c, 1.0, 0.0).astype(BF16)
    tot = _dot(hit.astype(BF16), before) + carry_ref[:, 0:1]
    rank1 = jnp.sum(jnp.where(erow == e1, tot, 0.0), axis=0, keepdims=True)
    rank2 = jnp.sum(jnp.where(erow == e2, tot, 0.0), axis=0, keepdims=True)
    carry_ref[...] = carry_ref[...] + jnp.sum(hit, axis=1, keepdims=True)
    cnt_ref[...] = carry_ref[...]
    info = jnp.zeros((SUBLANES, tm), F32)
    for row, val in ((INFO_E1, e1.astype(F32)), (INFO_E2, e2.astype(F32)), (INFO_R1, rank1),
                     (INFO_R2, rank2), (INFO_W1, w1), (INFO_W2, w2)):
        info = jnp.where(sub == row, val, info)
    info_ref[...] = info


DISPATCH_SLOTS = 3


def _dispatch_kernel(p1_ref, p2_ref, hn_ref, xs_ref, buf, lsem, ssem):
    i = pl.program_id(0)
    n = pl.num_programs(0)
    slab = xs_ref.shape[1]
    rows = buf.shape[1]
    tm = rows // slab
    slot = i % DISPATCH_SLOTS

    def load(step):
        sl = step % DISPATCH_SLOTS
        return pltpu.make_async_copy(hn_ref.at[pl.ds(step * rows, rows)], buf.at[sl], lsem.at[sl])

    def wait_scatters(sl):
        cp = pltpu.make_async_copy(buf.at[sl], hn_ref.at[pl.ds(0, rows)], ssem.at[sl])
        for _ in range(TOP_K):
            cp.wait()

    @pl.when(i == 0)
    def _():
        load(0).start()

    @pl.when(i + 1 < n)
    def _():
        load(i + 1).start()

    load(i).wait()

    def tok(r, carry):
        tt = i * tm + r
        src = buf.at[slot, pl.ds(pl.multiple_of(r * slab, slab), slab)]
        pltpu.make_async_copy(src, xs_ref.at[p1_ref[tt]], ssem.at[slot]).start()
        pltpu.make_async_copy(src, xs_ref.at[p2_ref[tt]], ssem.at[slot]).start()
        return carry

    lax.fori_loop(0, tm, tok, 0, unroll=8)

    @pl.when(i > 0)
    def _():
        wait_scatters((i - 1) % DISPATCH_SLOTS)

    @pl.when(i == n - 1)
    def _():
        wait_scatters(slot)


def _expert_kernel(ve_ref, vt_ref, lo_ref, hi_ref, nv_ref, x_ref, wg_ref, wu_ref, wd_ref, y_ref,
                   wgu_s, wd_s, gu_s):
    v = pl.program_id(0)
    d, f = wg_ref.shape[2], wg_ref.shape[3]
    tm = x_ref.shape[0] * LANES // d

    @pl.when(v < nv_ref[0])
    def _():
        pv = jnp.maximum(v - 1, 0)

        @pl.when((v == 0) | (ve_ref[v] != ve_ref[pv]))
        def _():
            wgu_s[:, :f] = wg_ref[0, 0].astype(BF16)
            wgu_s[:, f:] = wu_ref[0, 0].astype(BF16)
            wd_s[...] = wd_ref[0, 0].astype(BF16)

        half = tm // 2
        slab = d // LANES

        @pl.when((v == 0) | (vt_ref[v] != vt_ref[pv]))
        def _():
            y_ref[...] = jnp.zeros_like(y_ref)

        for hh in range(2):
            x = _load_slabs(x_ref.at[pl.ds(hh * half * slab, half * slab)], half, d)
            gu_s[hh] = _dot(x.astype(BF16), wgu_s[...])
        for hh in range(2):
            gate = gu_s[hh, :, :f]
            he = (gate * jax.nn.sigmoid(gate) * gu_s[hh, :, f:]).astype(BF16)
            row = lax.broadcasted_iota(I32, (half, 1), 0) + hh * half
            y = jnp.where((row >= lo_ref[v]) & (row < hi_ref[v]), _dot(he, wd_s[...]), 0.0)
            _store_slabs(y_ref.at[pl.ds(hh * half * slab, half * slab)], y, accumulate=True)


def _combine_kernel(p1_ref, p2_ref, h_ref, info_ref, fin_ref, ys_ref, ys2d_ref, *rest, final, with_mla):
    if with_mla:
        mla_in, (o_ref, q_ref, k_ref, vt_ref, buf, sem) = rest[:10], rest[10:]
    else:
        o_ref, buf, sem = rest
    i = pl.program_id(0)
    n = pl.num_programs(0)
    tm, d = h_ref.shape
    slab = ys_ref.shape[1]
    slot = i % 2

    def issue(step, sl):
        def tok(r, carry):
            tt = step * tm + r
            dst = pl.ds(pl.multiple_of(r * slab, slab), slab)
            pltpu.make_async_copy(ys_ref.at[p1_ref[tt]], buf.at[sl, 0, dst], sem.at[sl]).start()
            pltpu.make_async_copy(ys_ref.at[p2_ref[tt]], buf.at[sl, 1, dst], sem.at[sl]).start()
            return carry

        lax.fori_loop(0, tm, tok, 0, unroll=8)

    @pl.when(i == 0)
    def _():
        issue(0, 0)

    @pl.when(i + 1 < n)
    def _():
        issue(i + 1, 1 - slot)

    for k in range(TOP_K):
        pltpu.make_async_copy(ys2d_ref.at[pl.ds(0, tm * slab)], buf.at[slot, k], sem.at[slot]).wait()
    info = jnp.concatenate([info_ref[...], jnp.zeros((ROUTER_LANES - SUBLANES, tm), F32)], axis=0).T
    out = (h_ref[...] + info[:, INFO_W1:INFO_W1 + 1] * _load_slabs(buf.at[slot, 0], tm, d)
           + info[:, INFO_W2:INFO_W2 + 1] * _load_slabs(buf.at[slot, 1], tm, d))
    if final:
        out = _rms(out, fin_ref[...])
    o_ref[...] = out
    if with_mla:
        _mla_tile(out, *mla_in, q_ref, k_ref, vt_ref)


def _route_operands(g, w_grp, b_grp, w_exp, b_exp):
    d = g.shape[-1]
    zrow = lambda n: jnp.zeros((n, d), F32)
    wrt = jnp.concatenate([w_grp.T, zrow(ROUTER_EXPERT_ROW - N_GROUPS), w_exp.T,
                           zrow(ROUTER_ROWS - ROUTER_EXPERT_ROW - N_EXPERTS)], axis=0)
    brt = jnp.concatenate([b_grp, jnp.zeros((ROUTER_EXPERT_ROW - N_GROUPS,), F32), b_exp,
                           jnp.zeros((ROUTER_ROWS - ROUTER_EXPERT_ROW - N_EXPERTS,), F32)])
    brt = jnp.broadcast_to(brt[:, None], (ROUTER_ROWS, ROUTER_LANES))
    return g.reshape(1, d), wrt, brt


def _moe(h, routed, w_gate, w_up, w_down, layer, final_norm, final, mla=None):
    hn, info, cnt = routed
    b, s, d = h.shape
    t = b * s
    f = w_gate.shape[-1]
    slab = d // LANES
    assert slab == SUBLANES
    h2 = h.reshape(t, d)

    tm = EXPERT_TILE
    n_rows = TOP_K * t
    n_visits = n_rows // tm + N_EXPERTS - 1
    counts = cnt[:, 0].astype(I32)
    row_end = jnp.cumsum(counts)
    row_start = row_end - counts
    first_tile = row_start // tm
    e_visits = jnp.where(counts > 0, (row_end - 1) // tm - first_tile + 1, 0)
    v_end = jnp.cumsum(e_visits)
    v_start = v_end - e_visits
    n_v = v_end[-1:]
    vis = jnp.minimum(jnp.arange(n_visits, dtype=I32), n_v - 1)
    v_e = jnp.sum(vis[:, None] >= v_end[None, :], axis=1).astype(I32)
    v_tile = (first_tile[v_e] + vis - v_start[v_e]).astype(I32)
    v_lo = (jnp.maximum(row_start[v_e], v_tile * tm) - v_tile * tm).astype(I32)
    v_hi = (jnp.minimum(row_end[v_e], (v_tile + 1) * tm) - v_tile * tm).astype(I32)
    eid = jnp.arange(N_EXPERTS, dtype=I32)[:, None]
    start_of = lambda e: jnp.sum(jnp.where(e.astype(I32)[None, :] == eid, row_start[:, None], 0), axis=0)
    pos1 = start_of(info[INFO_E1]) + info[INFO_R1].astype(I32)
    pos2 = start_of(info[INFO_E2]) + info[INFO_R2].astype(I32)

    xs = pl.pallas_call(
        _dispatch_kernel,
        grid_spec=pltpu.PrefetchScalarGridSpec(
            num_scalar_prefetch=2, grid=(t // DISPATCH_TILE,),
            in_specs=[pl.BlockSpec(memory_space=pl.ANY)],
            out_specs=pl.BlockSpec(memory_space=pl.ANY),
            scratch_shapes=[pltpu.VMEM((DISPATCH_SLOTS, DISPATCH_TILE * slab, LANES), F32),
                            pltpu.SemaphoreType.DMA((DISPATCH_SLOTS,)),
                            pltpu.SemaphoreType.DMA((DISPATCH_SLOTS,))],
        ),
        out_shape=jax.ShapeDtypeStruct((n_rows, slab, LANES), F32),
        compiler_params=_params("arbitrary"),
        name="moe_dispatch",
    )(pos1, pos2, hn)

    by_tile = lambda v, ve, vt, lo, hi, nv: (vt[v], 0)
    by_expert = lambda v, ve, vt, lo, hi, nv: (layer, ve[v], 0, 0)
    ys = pl.pallas_call(
        _expert_kernel,
        grid_spec=pltpu.PrefetchScalarGridSpec(
            num_scalar_prefetch=5, grid=(n_visits,),
            in_specs=[
                pl.BlockSpec((tm * slab, LANES), by_tile),
                pl.BlockSpec((1, 1, d, f), by_expert),
                pl.BlockSpec((1, 1, d, f), by_expert),
                pl.BlockSpec((1, 1, f, d), by_expert),
            ],
            out_specs=pl.BlockSpec((tm * slab, LANES), by_tile),
            scratch_shapes=[pltpu.VMEM((d, 2 * f), BF16), pltpu.VMEM((f, d), BF16),
                            pltpu.VMEM((2, tm // 2, 2 * f), F32)],
        ),
        out_shape=jax.ShapeDtypeStruct((n_rows * slab, LANES), F32),
        compiler_params=_params("arbitrary"),
        name="moe_experts",
    )(v_e, v_tile, v_lo, v_hi, n_v, xs.reshape(n_rows * slab, LANES), w_gate, w_up, w_down)

    tc = COMBINE_TILE
    tok = lambda w: pl.BlockSpec((tc, w), lambda i, p1, p2: (i, 0))
    in_specs = [tok(d), pl.BlockSpec((SUBLANES, tc), lambda i, p1, p2: (0, i)),
                pl.BlockSpec((1, d), lambda i, p1, p2: (0, 0)),
                pl.BlockSpec(memory_space=pl.ANY), pl.BlockSpec(memory_space=pl.ANY)]
    args = [h2, info, final_norm.reshape(1, d), ys.reshape(n_rows, slab, LANES), ys]
    out_specs = tok(d)
    out_shape = jax.ShapeDtypeStruct((t, d), F32)
    if mla is not None:
        assert tc % MLA_TILE == 0 and s % tc == 0
        nblk = s // tc
        nsub = tc // MLA_TILE
        width = MLA_HEADS * HEAD_PAD
        in_specs += [pl.BlockSpec(w.shape, lambda i, p1, p2, nd=w.ndim: (0,) * nd) for w in mla[:7]]
        in_specs += [tok(HEAD_PAD)] * 3
        args += list(mla)
        out_specs = (out_specs, tok(width), tok(width),
                     pl.BlockSpec((1, MLA_HEADS, nsub, VT_ROWS, MLA_TILE),
                                  lambda i, p1, p2: (i // nblk, 0, i % nblk, 0, 0)))
        out_shape = (out_shape, jax.ShapeDtypeStruct((t, width), BF16), jax.ShapeDtypeStruct((t, width), BF16),
                     jax.ShapeDtypeStruct((b, MLA_HEADS, s // MLA_TILE, VT_ROWS, MLA_TILE), BF16))
    res = pl.pallas_call(
        functools.partial(_combine_kernel, final=final, with_mla=mla is not None),
        grid_spec=pltpu.PrefetchScalarGridSpec(
            num_scalar_prefetch=2, grid=(t // tc,),
            in_specs=in_specs,
            out_specs=out_specs,
            scratch_shapes=[pltpu.VMEM((2, TOP_K, tc * slab, LANES), F32), pltpu.SemaphoreType.DMA((2,))],
        ),
        out_shape=out_shape,
        compiler_params=_params("arbitrary"),
        name="moe_combine",
    )(pos1, pos2, *args)
    if mla is None:
        return res.reshape(b, s, d)
    out, q, k, vt = res
    return out.reshape(b, s, d), q.reshape(b, s, -1), k.reshape(b, s, -1), vt


MLA_TILE = 256
VT_ROWS = V_HEAD_DIM + 16


def _mla_tile(h, g_ref, wdn_ref, qn_ref, wuq_ref, kvn_ref, wuk_ref, wuvt_ref, tc_ref, ta_ref, tb_ref,
              q_ref, k_ref, vt_ref):
    hn = _rms(h, g_ref[...]).astype(BF16)
    dn = _dot(hn, wdn_ref[...])
    c, a, b = tc_ref[...], ta_ref[...], tb_ref[...]
    scale = LOG2_E / math.sqrt(QK_NOPE_DIM + QK_ROPE_DIM)
    kv_lo = Q_LORA_RANK
    kr_lo = Q_LORA_RANK + KV_LORA_RANK
    cq = _rms(dn[:, :kv_lo], qn_ref[...]).astype(BF16)
    q = _dot(cq, wuq_ref[...])
    for hh in range(MLA_HEADS):
        sl = slice(hh * HEAD_PAD, (hh + 1) * HEAD_PAD)
        q_ref[:, sl] = (_rope_apply(q[:, sl], c, a, b) * scale).astype(BF16)
    ckv = _rms(dn[:, kv_lo:kr_lo], kvn_ref[...]).astype(BF16)
    kn = _dot(ckv, wuk_ref[...])
    kr = _rope_apply(dn[:, kr_lo:kr_lo + HEAD_PAD], c, a, b)
    for hh in range(MLA_HEADS):
        sl = slice(hh * HEAD_PAD, (hh + 1) * HEAD_PAD)
        k_ref[:, sl] = (kn[:, sl] + kr).astype(BF16)
    vt = _dot_nt(wuvt_ref[...], ckv).astype(BF16)
    tk = vt_ref.shape[4]
    ones = jnp.ones((VT_ROWS - V_HEAD_DIM, tk), BF16)
    for hh in range(MLA_HEADS):
        for sb in range(vt_ref.shape[2]):
            vt_ref[0, hh, sb, 0:V_HEAD_DIM, :] = vt[hh * V_HEAD_DIM:(hh + 1) * V_HEAD_DIM, sb * tk:(sb + 1) * tk]
            vt_ref[0, hh, sb, V_HEAD_DIM:VT_ROWS, :] = ones


def _mla_operands(d, g, w_down, q_norm, w_uq, kv_norm, w_ukv, tabs):
    nh, dn_, dr, dv = MLA_HEADS, QK_NOPE_DIM, QK_ROPE_DIM, V_HEAD_DIM
    kr_lo = Q_LORA_RANK + KV_LORA_RANK
    w_dn = jnp.concatenate([
        w_down[:, :kr_lo], jnp.zeros((d, dn_), F32), w_down[:, kr_lo:],
        jnp.zeros((d, HEAD_PAD - dn_ - dr), F32)], axis=1).astype(BF16)
    wq = w_uq.reshape(Q_LORA_RANK, nh, dn_ + dr)
    wq = jnp.pad(wq, ((0, 0), (0, 0), (0, HEAD_PAD - dn_ - dr))).reshape(Q_LORA_RANK, nh * HEAD_PAD).astype(BF16)
    wkv = w_ukv.reshape(KV_LORA_RANK, nh, dn_ + dv)
    wk = jnp.pad(wkv[:, :, :dn_], ((0, 0), (0, 0), (0, HEAD_PAD - dn_))).reshape(KV_LORA_RANK, nh * HEAD_PAD).astype(BF16)
    wvt = wkv[:, :, dn_:].reshape(KV_LORA_RANK, nh * dv).T.astype(BF16)
    return (g.reshape(1, d), w_dn, q_norm.reshape(1, -1), wq, kv_norm.reshape(1, -1), wk, wvt) + tuple(tabs)


FLASH_HEADS = 4


def _flash_kernel(q_ref, k_ref, vt_ref, o_ref, sa_ref, sb_ref, acc_ref):
    tq = q_ref.shape[1]
    tk = vt_ref.shape[4]
    qi = pl.program_id(2)

    def scores(j, s_ref, c0=0):
        k2 = k_ref[0, pl.ds(pl.multiple_of(j * tk, tk), tk), :]
        for hh in range(FLASH_HEADS):
            sl = slice(hh * HEAD_PAD, (hh + 1) * HEAD_PAD)
            s_ref[hh, :, c0:] = _dot_nt(k2[:, sl], q_ref[0, c0:, sl])

    def consume(j, s_ref, ms, stream, c0=0, diag=False):
        out = []
        for hh in range(FLASH_HEADS):
            st = s_ref[hh, :, c0:]
            if diag:
                key = lax.broadcasted_iota(I32, st.shape, 0)
                qry = lax.broadcasted_iota(I32, st.shape, 1)
                st = jnp.where(key <= qry, st, NEG_BIG)
            m = ms[hh][:, c0:]
            m_new = jnp.maximum(m, jnp.max(st, axis=0, keepdims=True))
            alpha = jnp.exp2(m - m_new)
            p = jnp.exp2(st - m_new).astype(BF16)
            acc_ref[stream, hh, :, c0:] = alpha * acc_ref[stream, hh, :, c0:] + _dot(vt_ref[0, hh, j], p)
            out.append(m_new if c0 == 0 else jnp.concatenate([ms[hh][:, :c0], m_new], axis=1))
        return tuple(out)

    acc_ref[...] = jnp.zeros_like(acc_ref)
    ms0 = tuple(jnp.full((1, tq), NEG_BIG, F32) for _ in range(2 * FLASH_HEADS))
    scores(0, sa_ref)

    def two_tiles(i, ms):
        scores(2 * i + 1, sb_ref)
        even = consume(2 * i, sa_ref, ms[:FLASH_HEADS], 0)
        scores(2 * i + 2, sa_ref)
        return even + consume(2 * i + 1, sb_ref, ms[FLASH_HEADS:], 1)

    assert tq == 2 * tk
    ms = lax.fori_loop(0, qi, two_tiles, ms0)
    scores(2 * qi + 1, sb_ref, c0=tk)
    even = consume(2 * qi, sa_ref, ms[:FLASH_HEADS], 0, diag=True)
    odd = consume(2 * qi + 1, sb_ref, ms[FLASH_HEADS:], 1, c0=tk, diag=True)
    heads = []
    for hh in range(FLASH_HEADS):
        m = jnp.maximum(even[hh], odd[hh])
        acc = jnp.exp2(even[hh] - m) * acc_ref[0, hh] + jnp.exp2(odd[hh] - m) * acc_ref[1, hh]
        heads.append(acc[0:V_HEAD_DIM, :] * (1.0 / acc[V_HEAD_DIM:V_HEAD_DIM + 1, :]))
    o_ref[0] = jnp.concatenate(heads, axis=0).T.astype(BF16)


def _flash(q, k, vt, tq=512):
    b, s, _ = q.shape
    _, nh, nblk, rows, tk = vt.shape
    return pl.pallas_call(
        _flash_kernel,
        grid=(b, nh // FLASH_HEADS, s // tq),
        in_specs=[
            pl.BlockSpec((1, tq, FLASH_HEADS * HEAD_PAD), lambda i, p, j: (i, j, p)),
            pl.BlockSpec((1, s, FLASH_HEADS * HEAD_PAD), lambda i, p, j: (i, 0, p)),
            pl.BlockSpec((1, FLASH_HEADS, nblk, rows, tk), lambda i, p, j: (i, p, 0, 0, 0)),
        ],
        out_specs=pl.BlockSpec((1, tq, FLASH_HEADS * V_HEAD_DIM), lambda i, p, j: (i, j, p)),
        out_shape=jax.ShapeDtypeStruct((b, s, nh * V_HEAD_DIM), BF16),
        scratch_shapes=[pltpu.VMEM((FLASH_HEADS, tk, tq), F32), pltpu.VMEM((FLASH_HEADS, tk, tq), F32),
                        pltpu.VMEM((2, FLASH_HEADS, rows, tq), F32)],
        compiler_params=_params("arbitrary", "arbitrary", "arbitrary"),
        name="flash",
    )(q, k, vt)


def kernel(x, mem, positions, norm_mix, conv_w_in, conv_b_in, conv_w_dw, conv_b_dw, conv_ln_g, conv_ln_b, conv_w_out, conv_b_out, mla_w_down, mla_q_norm, mla_w_uq, mla_kv_norm, mla_w_ukv, mla_w_o, norm_xa, norm_mem, xa_wq, xa_wkv, xa_wo, norm_ffn, moe_w_grp, moe_b_grp, moe_w_exp, moe_b_exp, moe_w_gate, moe_w_up, moe_w_down, final_norm):
    tabs = _rope_tables(positions)
    kv_mem = _mem_kv(mem, norm_mem, xa_wkv.astype(BF16))

    def route(i):
        return _route_operands(norm_ffn[i], moe_w_grp[i], moe_b_grp[i], moe_w_exp[i], moe_b_exp[i])

    def ffn(h, routed, i, final, mla=None):
        return _moe(h, routed, moe_w_gate, moe_w_up, moe_w_down, i, final_norm, final, mla)

    h = _conv_mixer(x, norm_mix[0], conv_w_in[0], conv_b_in[0], conv_w_dw[0], conv_b_dw[0],
                    conv_ln_g[0], conv_ln_b[0], conv_w_out[0], conv_b_out[0])
    h, *routed = _xattn(h, norm_xa[0], xa_wq[0], kv_mem, 0, xa_wo[0], route(0))
    mla = _mla_operands(x.shape[-1], norm_mix[1], mla_w_down[0], mla_q_norm[0], mla_w_uq[0], mla_kv_norm[0],
                        mla_w_ukv[0], tabs)
    h, q, k, vt = ffn(h, routed, 0, False, mla)
    attn = _flash(q, k, vt)
    h, *routed = _xattn(h, norm_xa[1], xa_wq[1], kv_mem, 1, xa_wo[1], route(1), prefix=(attn, mla_w_o[0]))
    return ffn(h, routed, 1, True)
```
